```python
import jax, jax.numpy as jnp
from jax import lax
import numpy as np

D_MODEL = 4096
BATCH = 2
SEQ = 4096
DEPTH = 2

BRANCH_WIDTH = D_MODEL // 4
N_BRANCHES = 3
CONV_CH = BRANCH_WIDTH
CONV_K = 3
MLA_NOPE = 128
MLA_ROPE = 64
MLA_V = 128
MLA_HEADS = BRANCH_WIDTH // MLA_V
MLA_Q_LORA = 3 * D_MODEL // 16
MLA_KV_LORA = D_MODEL // 8
ROPE_THETA = 10000.0
FOX_HEAD_DIM = 128
FOX_HEADS = BRANCH_WIDTH // FOX_HEAD_DIM
Q_BLOCK = 128
MIX_COLS = 3 * CONV_CH + MLA_Q_LORA + MLA_KV_LORA + MLA_ROPE + 3 * BRANCH_WIDTH + FOX_HEADS
N_EXPERTS = 32
TOP_K = 4
D_EXPERT = D_MODEL // 8
SWIGLU_ALPHA = 1.702
SWIGLU_LIMIT = 7.0
EXPERT_BLOCK = 128
N_ADA = 6
NORM_EPS = 1e-6

kernel_name = "hybrid_conv_mla_fox_moe_block"


def rms_norm(x, g):
    xf = x.astype(jnp.float32)
    y = xf * lax.rsqrt(jnp.mean(xf * xf, axis=-1, keepdims=True) + NORM_EPS)
    return (y * g.astype(jnp.float32)).astype(x.dtype)


def rope(x, positions):
    half = x.shape[-1] // 2
    inv_freq = ROPE_THETA ** (-jnp.arange(half, dtype=jnp.float32) / half)
    ang = positions.astype(jnp.float32)[..., None] * inv_freq
    cos = jnp.cos(ang)[:, :, None, :]
    sin = jnp.sin(ang)[:, :, None, :]
    x1 = x[..., :half].astype(jnp.float32)
    x2 = x[..., half:].astype(jnp.float32)
    out = jnp.concatenate([x1 * cos - x2 * sin, x2 * cos + x1 * sin], axis=-1)
    return out.astype(x.dtype)


def block_causal_attention(q, k, v, scale, cum_log_f=None):
    B, S, H, Dk = q.shape
    Dv = v.shape[-1]
    nb = S // Q_BLOCK
    qb = q.reshape(B, nb, Q_BLOCK, H, Dk).swapaxes(0, 1)
    key_pos = jnp.arange(S)
    xs = (jnp.arange(nb), qb)
    if cum_log_f is not None:
        cum_k = cum_log_f.transpose(0, 2, 1)
        cqb = cum_log_f.reshape(B, nb, Q_BLOCK, H).swapaxes(0, 1)
        xs = xs + (cqb,)

    def one_block(args):
        i, q_i = args[0], args[1]
        s = jnp.einsum('bqhd,bkhd->bhqk', q_i, k,
                       preferred_element_type=jnp.float32) * scale
        if cum_log_f is not None:
            cq = args[2].transpose(0, 2, 1)
            s = s + (cq[..., None] - cum_k[:, :, None, :])
        q_pos = i * Q_BLOCK + jnp.arange(Q_BLOCK)
        mask = key_pos[None, :] <= q_pos[:, None]
        s = jnp.where(mask, s, -jnp.inf)
        p = jax.nn.softmax(s, axis=-1)
        return jnp.einsum('bhqk,bkhd->bqhd', p.astype(v.dtype), v)

    out = lax.map(one_block, xs)
    return out.swapaxes(0, 1).reshape(B, S, H, Dv)


def hybrid_mixer(h, positions, w_mix_in, conv_w, q_norm_g, kv_norm_g, w_uq, w_ukv,
                 fox_f_bias, w_branch, w_merge_gate, b_merge_gate, w_o):
    B, S, _ = h.shape
    sizes = (CONV_CH, CONV_CH, CONV_CH, MLA_Q_LORA, MLA_KV_LORA, MLA_ROPE,
             BRANCH_WIDTH, BRANCH_WIDTH, BRANCH_WIDTH, FOX_HEADS)
    split_at = np.cumsum(sizes)[:-1].tolist()
    u = h @ w_mix_in
    (conv_b, conv_c, conv_h, c_q, c_kv, k_rope_in,
     fox_q, fox_k, fox_v, fox_f) = jnp.split(u, split_at, axis=-1)

    gated_in = conv_c * conv_h
    padded = jnp.pad(gated_in, ((0, 0), (CONV_K - 1, 0), (0, 0)))
    conv_out = (conv_w[0] * padded[:, 0:S] + conv_w[1] * padded[:, 1:S + 1]
                + conv_w[2] * padded[:, 2:S + 2])
    o_a = conv_b * conv_out

    q = (rms_norm(c_q, q_norm_g) @ w_uq).reshape(B, S, MLA_HEADS, MLA_NOPE + MLA_ROPE)
    q = jnp.concatenate([q[..., :MLA_NOPE], rope(q[..., MLA_NOPE:], positions)], axis=-1)
    kv = (rms_norm(c_kv, kv_norm_g) @ w_ukv).reshape(B, S, MLA_HEADS, MLA_NOPE + MLA_V)
    k_r = rope(k_rope_in[:, :, None, :], positions)
    k = jnp.concatenate([kv[..., :MLA_NOPE],
                         jnp.broadcast_to(k_r, (B, S, MLA_HEADS, MLA_ROPE))], axis=-1)
    v_mla = kv[..., MLA_NOPE:]
    o_b = block_causal_attention(q, k, v_mla, (MLA_NOPE + MLA_ROPE) ** -0.5)
    o_b = o_b.reshape(B, S, MLA_HEADS * MLA_V)

    log_f = jax.nn.log_sigmoid(fox_f.astype(jnp.float32) + fox_f_bias.astype(jnp.float32))
    cum_log_f = jnp.cumsum(log_f, axis=1)
    qf = fox_q.reshape(B, S, FOX_HEADS, FOX_HEAD_DIM)
    kf = fox_k.reshape(B, S, FOX_HEADS, FOX_HEAD_DIM)
    vf = fox_v.reshape(B, S, FOX_HEADS, FOX_HEAD_DIM)
    o_c = block_causal_attention(qf, kf, vf, FOX_HEAD_DIM ** -0.5, cum_log_f)
    o_c = o_c.reshape(B, S, FOX_HEADS * FOX_HEAD_DIM)

    branches = (o_a, o_b, o_c)
    merged = jax.nn.sigmoid(h @ w_merge_gate[0] + b_merge_gate[0]) * (branches[0] @ w_branch[0])
    for n in range(1, N_BRANCHES):
        gate = jax.nn.sigmoid(h @ w_merge_gate[n] + b_merge_gate[n])
        merged = merged + gate * (branches[n] @ w_branch[n])
    return merged @ w_o


def moe_ffn(h, w_router, b_router, w_gate_up, b_gate_up, w_down, b_down):
    B, S, D = h.shape
    n_tok = B * S
    nk = n_tok * TOP_K
    t = h.reshape(n_tok, D)
    logits = (t @ w_router).astype(jnp.float32) + b_router.astype(jnp.float32)
    top_logit, top_e = lax.top_k(logits, TOP_K)
    top_p = jax.nn.softmax(top_logit, axis=-1)

    flat_e = top_e.reshape(-1)
    order = jnp.argsort(flat_e)
    e_sorted = flat_e[order]
    tok_sorted = order // TOP_K
    p_sorted = top_p.reshape(-1)[order]

    counts = jnp.bincount(flat_e, length=N_EXPERTS)
    padded = (counts + EXPERT_BLOCK - 1) // EXPERT_BLOCK * EXPERT_BLOCK
    start = jnp.cumsum(counts) - counts
    pad_end = jnp.cumsum(padded)
    pad_start = pad_end - padded
    dest = pad_start[e_sorted] + (jnp.arange(nk) - start[e_sorted])

    n_blocks = (nk + N_EXPERTS * (EXPERT_BLOCK - 1) + EXPERT_BLOCK - 1) // EXPERT_BLOCK
    n_rows = n_blocks * EXPERT_BLOCK
    x_grouped = jnp.zeros((n_rows, D), t.dtype).at[dest].set(t[tok_sorted])
    block_e = jnp.minimum(
        jnp.searchsorted(pad_end, jnp.arange(n_blocks) * EXPERT_BLOCK, side='right'),
        N_EXPERTS - 1)

    def expert_block(args):
        xb, e = args
        gu = xb @ w_gate_up[e] + b_gate_up[e]
        g = jnp.minimum(gu[..., :D_EXPERT], SWIGLU_LIMIT)
        up = jnp.clip(gu[..., D_EXPERT:], -SWIGLU_LIMIT, SWIGLU_LIMIT)
        act = g * jax.nn.sigmoid(SWIGLU_ALPHA * g) * (up + 1.0)
        return act @ w_down[e] + b_down[e]

    y_grouped = lax.map(expert_block,
                        (x_grouped.reshape(n_blocks, EXPERT_BLOCK, D), block_e))
    y_sel = y_grouped.reshape(n_rows, D)[dest] * p_sorted.astype(t.dtype)[:, None]
    out = jnp.zeros((n_tok, D), t.dtype).at[tok_sorted].add(y_sel)
    return out.reshape(B, S, D)


def setup_inputs(seed: int = 0) -> dict:
    key = jax.random.key(seed)
    ks = jax.random.split(key, 32)
    L, D = DEPTH, D_MODEL
    f32 = jnp.float32

    def nrm(k, shape, scale):
        return jax.random.normal(k, shape, f32) * scale

    x = jax.random.normal(ks[0], (BATCH, SEQ, D), f32)
    c = jax.random.normal(ks[1], (BATCH, D), f32)
    positions = (jax.random.randint(ks[2], (BATCH, 1), 0, 1024)
                 + jnp.arange(SEQ)[None, :]).astype(jnp.int32)
    return {
        "x": x,
        "c": c,
        "positions": positions,
        "w_ada": nrm(ks[3], (L, D, N_ADA * D), 0.5 * D ** -0.5),
        "b_ada": nrm(ks[4], (L, N_ADA * D), 0.02),
        "g_mix_pre": 1.0 + nrm(ks[5], (L, D), 0.05),
        "g_mix_post": 1.0 + nrm(ks[6], (L, D), 0.05),
        "g_ffn_pre": 1.0 + nrm(ks[7], (L, D), 0.05),
        "g_ffn_post": 1.0 + nrm(ks[8], (L, D), 0.05),
        "w_mix_in": nrm(ks[9], (L, D, MIX_COLS), D ** -0.5),
        "conv_w": nrm(ks[10], (L, CONV_K, CONV_CH), CONV_K ** -0.5),
        "q_norm_g": 1.0 + nrm(ks[11], (L, MLA_Q_LORA), 0.05),
        "kv_norm_g": 1.0 + nrm(ks[12], (L, MLA_KV_LORA), 0.05),
        "w_uq": nrm(ks[13], (L, MLA_Q_LORA, MLA_HEADS * (MLA_NOPE + MLA_ROPE)), MLA_Q_LORA ** -0.5),
        "w_ukv": nrm(ks[14], (L, MLA_KV_LORA, MLA_HEADS * (MLA_NOPE + MLA_V)), MLA_KV_LORA ** -0.5),
        "fox_f_bias": jax.random.uniform(ks[15], (L, FOX_HEADS), f32, 1.0, 4.0),
        "w_branch": nrm(ks[16], (L, N_BRANCHES, BRANCH_WIDTH, D), BRANCH_WIDTH ** -0.5),
        "w_merge_gate": nrm(ks[17], (L, N_BRANCHES, D, D), D ** -0.5),
        "b_merge_gate": nrm(ks[18], (L, N_BRANCHES, D), 0.02),
        "w_o": nrm(ks[19], (L, D, D), D ** -0.5),
        "w_router": nrm(ks[20], (L, D, N_EXPERTS), D ** -0.5),
        "b_router": nrm(ks[21], (L, N_EXPERTS), 0.01),
        "w_gate_up": nrm(ks[22], (L, N_EXPERTS, D, 2 * D_EXPERT), D ** -0.5),
        "b_gate_up": nrm(ks[23], (L, N_EXPERTS, 2 * D_EXPERT), 0.02),
        "w_down": nrm(ks[24], (L, N_EXPERTS, D_EXPERT, D), D_EXPERT ** -0.5),
        "b_down": nrm(ks[25], (L, N_EXPERTS, D), 0.02),
    }


def reference(x, c, positions, w_ada, b_ada, g_mix_pre, g_mix_post, g_ffn_pre, g_ffn_post,
              w_mix_in, conv_w, q_norm_g, kv_norm_g, w_uq, w_ukv, fox_f_bias, w_branch,
              w_merge_gate, b_merge_gate, w_o, w_router, b_router, w_gate_up, b_gate_up,
              w_down, b_down):
    B = x.shape[0]
    c_act = jax.nn.silu(c)
    for l in range(DEPTH):
        ada = (c_act @ w_ada[l] + b_ada[l]).reshape(B, N_ADA, D_MODEL)
        shift_m, scale_m, gate_m = ada[:, 0, None], ada[:, 1, None], ada[:, 2, None]
        shift_f, scale_f, gate_f = ada[:, 3, None], ada[:, 4, None], ada[:, 5, None]

        h = rms_norm(x, g_mix_pre[l]) * (1.0 + scale_m) + shift_m
        y = hybrid_mixer(h, positions, w_mix_in[l], conv_w[l], q_norm_g[l], kv_norm_g[l],
                         w_uq[l], w_ukv[l], fox_f_bias[l], w_branch[l], w_merge_gate[l],
                         b_merge_gate[l], w_o[l])
        x = x + gate_m * rms_norm(y, g_mix_post[l])

        h = rms_norm(x, g_ffn_pre[l]) * (1.0 + scale_f) + shift_f
        y = moe_ffn(h, w_router[l], b_router[l], w_gate_up[l], b_gate_up[l],
                    w_down[l], b_down[l])
        x = x + gate_f * rms_norm(y, g_ffn_post[l])
    return x
```

```python
import functools

import jax
import jax.numpy as jnp
from jax import lax
from jax.experimental import pallas as pl
from jax.experimental.pallas import tpu as pltpu

NORM_EPS = 1e-6
ROPE_THETA = 10000.0
HEAD_DIM = 128
ROPE_DIM = 64
CONV_K = 3
TOP_K = 4
SWIGLU_ALPHA = 1.702
SWIGLU_LIMIT = 7.0
N_ADA = 6
LANES = 128
QK_PAD = 2 * LANES
EXPERT_ROWS = 256
MIB = 1024 * 1024

F32 = jnp.float32
BF16 = jnp.bfloat16


def _cparams(sem, vmem_mib=48):
    return pltpu.CompilerParams(dimension_semantics=sem, vmem_limit_bytes=vmem_mib * MIB)


def _tile(n, pref):
    t = min(pref, n)
    while n % t:
        t //= 2
    return t


def _split3(x):
    hi = x.astype(BF16)
    r1 = x - hi.astype(F32)
    mid = r1.astype(BF16)
    lo = (r1 - mid.astype(F32)).astype(BF16)
    return hi, mid, lo


def _ada_kernel(c_ref, w_ref, b_ref, o_ref):
    c = c_ref[...]
    c_act = (c * jax.nn.sigmoid(c)).astype(BF16)
    acc = jnp.dot(c_act, w_ref[0].astype(BF16), preferred_element_type=F32)
    o_ref[0] = acc + b_ref[0]


def _ada(c8, w_ada, b_ada):
    L, D, N = w_ada.shape
    tn = _tile(N, 1024)
    return pl.pallas_call(
        _ada_kernel,
        grid=(L, N // tn),
        in_specs=[
            pl.BlockSpec((8, D), lambda l, j: (0, 0)),
            pl.BlockSpec((1, D, tn), lambda l, j: (l, 0, j)),
            pl.BlockSpec((1, 1, tn), lambda l, j: (l, 0, j)),
        ],
        out_specs=pl.BlockSpec((1, 8, tn), lambda l, j: (l, 0, j)),
        out_shape=jax.ShapeDtypeStruct((L, 8, N), F32),
        compiler_params=_cparams(("parallel", "parallel"), 56),
        name="ada",
    )(c8, w_ada, b_ada.reshape(L, 1, N))


def _norm_mod(x, g, scale, shift):
    xf = x
    y = xf * lax.rsqrt(jnp.mean(xf * xf, axis=-1, keepdims=True) + NORM_EPS)
    return (y * g) * (1.0 + scale) + shift


def _norm_mod_kernel(x_ref, g_ref, sc_ref, sh_ref, o_ref):
    o_ref[0] = _norm_mod(x_ref[0], g_ref[...], sc_ref[0], sh_ref[0]).astype(o_ref.dtype)


def _norm_mod_call(x, g, scale, shift):
    B, S, D = x.shape
    ts = _tile(S, 256)
    return pl.pallas_call(
        _norm_mod_kernel,
        grid=(B, S // ts),
        in_specs=[
            pl.BlockSpec((1, ts, D), lambda b, i: (b, i, 0)),
            pl.BlockSpec((1, D), lambda b, i: (0, 0)),
            pl.BlockSpec((1, 1, D), lambda b, i: (b, 0, 0)),
            pl.BlockSpec((1, 1, D), lambda b, i: (b, 0, 0)),
        ],
        out_specs=pl.BlockSpec((1, ts, D), lambda b, i: (b, i, 0)),
        out_shape=jax.ShapeDtypeStruct((B, S, D), BF16),
        compiler_params=_cparams(("parallel", "parallel")),
        name="norm_mod",
    )(x, g.reshape(1, D), scale, shift)


def _norm_router_kernel(n_experts, x_ref, g_ref, sc_ref, sh_ref, wr_hi_ref, wr_lo_ref, br_ref,
                        h_ref, e_ref, p_ref):
    h = _norm_mod(x_ref[0], g_ref[...], sc_ref[0], sh_ref[0])
    h_ref[0] = h
    h_hi = h.astype(BF16)
    h_lo = (h - h_hi.astype(F32)).astype(BF16)
    logits = (jnp.dot(h_hi, wr_hi_ref[...], preferred_element_type=F32)
              + jnp.dot(h_hi, wr_lo_ref[...], preferred_element_type=F32)
              + jnp.dot(h_lo, wr_hi_ref[...], preferred_element_type=F32)) + br_ref[...]
    lane = lax.broadcasted_iota(jnp.int32, logits.shape, 1)
    vals = jnp.where(lane < n_experts, logits, -jnp.inf)
    tops, idxs = [], []
    for _ in range(TOP_K):
        m = jnp.max(vals, axis=-1, keepdims=True)
        idx = jnp.min(jnp.where(vals == m, lane, LANES), axis=-1, keepdims=True)
        tops.append(m)
        idxs.append(idx)
        vals = jnp.where(lane == idx, -jnp.inf, vals)
    exps = [jnp.exp(t - tops[0]) for t in tops]
    denom = exps[0]
    for e in exps[1:]:
        denom = denom + e
    e_out = jnp.zeros(logits.shape, jnp.int32)
    p_out = jnp.zeros(logits.shape, F32)
    for k in range(TOP_K):
        e_out = jnp.where(lane == k, idxs[k], e_out)
        p_out = jnp.where(lane == k, exps[k] / denom, p_out)
    e_ref[...] = e_out
    p_ref[...] = p_out


def _norm_router_call(x, g, scale, shift, w_router, b_router):
    B, S, D = x.shape
    E = w_router.shape[1]
    ts = _tile(S, 256)
    nb = S // ts
    wr = jnp.pad(w_router, ((0, 0), (0, LANES - E)))
    wr_hi = wr.astype(BF16)
    wr_lo = (wr - wr_hi.astype(F32)).astype(BF16)
    br = jnp.pad(b_router, (0, LANES - E)).reshape(1, LANES)
    return pl.pallas_call(
        functools.partial(_norm_router_kernel, E),
        grid=(B, nb),
        in_specs=[
            pl.BlockSpec((1, ts, D), lambda b, i: (b, i, 0)),
            pl.BlockSpec((1, D), lambda b, i: (0, 0)),
            pl.BlockSpec((1, 1, D), lambda b, i: (b, 0, 0)),
            pl.BlockSpec((1, 1, D), lambda b, i: (b, 0, 0)),
            pl.BlockSpec((D, LANES), lambda b, i: (0, 0)),
            pl.BlockSpec((D, LANES), lambda b, i: (0, 0)),
            pl.BlockSpec((1, LANES), lambda b, i: (0, 0)),
        ],
        out_specs=[
            pl.BlockSpec((1, ts, D), lambda b, i: (b, i, 0)),
            pl.BlockSpec((ts, LANES), lambda b, i: (b * nb + i, 0)),
            pl.BlockSpec((ts, LANES), lambda b, i: (b * nb + i, 0)),
        ],
        out_shape=[
            jax.ShapeDtypeStruct((B, S, D), F32),
            jax.ShapeDtypeStruct((B * S, LANES), jnp.int32),
            jax.ShapeDtypeStruct((B * S, LANES), F32),
        ],
        compiler_params=_cparams(("parallel", "parallel")),
        name="norm_router",
    )(x, g.reshape(1, D), scale, shift, wr_hi, wr_lo, br)


def _mm_kernel(a_ref, w_ref, o_ref):
    o_ref[...] = jnp.dot(a_ref[...], w_ref[...], preferred_element_type=F32).astype(o_ref.dtype)


def _mm(a, w, out_dtype, tm_pref=1024, tn_pref=512, name="mm"):
    M, K = a.shape
    N = w.shape[1]
    tm = _tile(M, tm_pref)
    tn = _tile(N, tn_pref)
    return pl.pallas_call(
        _mm_kernel,
        grid=(M // tm, N // tn),
        in_specs=[
            pl.BlockSpec((tm, K), lambda i, j: (i, 0)),
            pl.BlockSpec((K, tn), lambda i, j: (0, j)),
        ],
        out_specs=pl.BlockSpec((tm, tn), lambda i, j: (i, j)),
        out_shape=jax.ShapeDtypeStruct((M, N), out_dtype),
        compiler_params=_cparams(("parallel", "parallel"), 56),
        name=name,
    )(a, w)


def _conv_kernel(b_ref, c_ref, h_ref, w_ref, o_ref):
    g = c_ref[...].astype(F32) * h_ref[...].astype(F32)
    row = lax.broadcasted_iota(jnp.int32, g.shape, 0)
    g1 = jnp.where(row >= 1, pltpu.roll(g, 1, 0), 0.0)
    g2 = jnp.where(row >= 2, pltpu.roll(g, 2, 0), 0.0)
    w = w_ref[...]
    conv = w[0:1] * g2 + w[1:2] * g1 + w[2:3] * g
    o_ref[...] = (b_ref[...].astype(F32) * conv).astype(o_ref.dtype)


def _conv_call(u, conv_w, B, S, C):
    tc = _tile(C, 256)
    nc = C // tc
    w8 = jnp.pad(conv_w, ((0, 8 - CONV_K), (0, 0)))
    return pl.pallas_call(
        _conv_kernel,
        grid=(B, nc),
        in_specs=[
            pl.BlockSpec((S, tc), lambda b, j: (b, j)),
            pl.BlockSpec((S, tc), lambda b, j: (b, nc + j)),
            pl.BlockSpec((S, tc), lambda b, j: (b, 2 * nc + j)),
            pl.BlockSpec((8, tc), lambda b, j: (0, j)),
        ],
        out_specs=pl.BlockSpec((S, tc), lambda b, j: (b, j)),
        out_shape=jax.ShapeDtypeStruct((B * S, C), BF16),
        compiler_params=_cparams(("parallel", "parallel")),
        name="gated_conv",
    )(u, u, u, w8)


def _rms(x, g):
    return x * lax.rsqrt(jnp.mean(x * x, axis=-1, keepdims=True) + NORM_EPS) * g


def _mla_q_kernel(n_heads, h_ref, wc_ref, g_ref, wa_ref, wb_ref, cos_ref, sin_ref, q_ref):
    c_q = jnp.dot(h_ref[...], wc_ref[...], preferred_element_type=F32)
    n = _rms(c_q, g_ref[...]).astype(BF16)
    a = jnp.dot(n, wa_ref[...], preferred_element_type=F32)
    bm = jnp.dot(n, wb_ref[...], preferred_element_type=F32)
    cos = cos_ref[...]
    sin = sin_ref[...]
    for hd in range(n_heads):
        o = hd * QK_PAD
        q_ref[:, o:o + LANES] = a[:, o:o + LANES].astype(q_ref.dtype)
        rot = a[:, o + LANES:o + QK_PAD] * cos + bm[:, hd * LANES:(hd + 1) * LANES] * sin
        q_ref[:, o + LANES:o + QK_PAD] = rot.astype(q_ref.dtype)


def _mla_q_call(h, w_cq, q_norm_g, w_a, w_b, cos_t, sin_t, n_heads):
    M, D = h.shape
    R = w_cq.shape[1]
    tm = _tile(M, 512)
    const = lambda i: (0, 0)
    return pl.pallas_call(
        functools.partial(_mla_q_kernel, n_heads),
        grid=(M // tm,),
        in_specs=[
            pl.BlockSpec((tm, D), lambda i: (i, 0)),
            pl.BlockSpec((D, R), const),
            pl.BlockSpec((1, R), const),
            pl.BlockSpec((R, n_heads * QK_PAD), const),
            pl.BlockSpec((R, n_heads * LANES), const),
            pl.BlockSpec((tm, LANES), lambda i: (i, 0)),
            pl.BlockSpec((tm, LANES), lambda i: (i, 0)),
        ],
        out_specs=pl.BlockSpec((tm, n_heads * QK_PAD), lambda i: (i, 0)),
        out_shape=jax.ShapeDtypeStruct((M, n_heads * QK_PAD), BF16),
        compiler_params=_cparams(("parallel",), 56),
        name="mla_q",
    )(h, w_cq, q_norm_g.reshape(1, R), w_a, w_b, cos_t, sin_t)


def _mla_kv_kernel(n_heads, h_ref, wc_ref, g_ref, wk_ref, wv_ref, wra_ref, wrb_ref,
                   cos_ref, sin_ref, k_ref, v_ref):
    hb = h_ref[...]
    c_kv = jnp.dot(hb, wc_ref[...], preferred_element_type=F32)
    n = _rms(c_kv, g_ref[...]).astype(BF16)
    k_nope = jnp.dot(n, wk_ref[...], preferred_element_type=F32)
    v_ref[...] = jnp.dot(n, wv_ref[...], preferred_element_type=F32).astype(v_ref.dtype)
    ra = jnp.dot(hb, wra_ref[...], preferred_element_type=F32)
    rb = jnp.dot(hb, wrb_ref[...], preferred_element_type=F32)
    k_rope = (ra * cos_ref[...] + rb * sin_ref[...]).astype(k_ref.dtype)
    for hd in range(n_heads):
        o = hd * QK_PAD
        k_ref[:, o:o + LANES] = k_nope[:, hd * LANES:(hd + 1) * LANES].astype(k_ref.dtype)
        k_ref[:, o + LANES:o + QK_PAD] = k_rope


def _mla_kv_call(h, w_ckv, kv_norm_g, w_k, w_v, w_ra, w_rb, cos_t, sin_t, n_heads):
    M, D = h.shape
    R = w_ckv.shape[1]
    tm = _tile(M, 512)
    const = lambda i: (0, 0)
    return pl.pallas_call(
        functools.partial(_mla_kv_kernel, n_heads),
        grid=(M // tm,),
        in_specs=[
            pl.BlockSpec((tm, D), lambda i: (i, 0)),
            pl.BlockSpec((D, R), const),
            pl.BlockSpec((1, R), const),
            pl.BlockSpec((R, n_heads * LANES), const),
            pl.BlockSpec((R, n_heads * LANES), const),
            pl.BlockSpec((D, LANES), const),
            pl.BlockSpec((D, LANES), const),
            pl.BlockSpec((tm, LANES), lambda i: (i, 0)),
            pl.BlockSpec((tm, LANES), lambda i: (i, 0)),
        ],
        out_specs=[
            pl.BlockSpec((tm, n_heads * QK_PAD), lambda i: (i, 0)),
            pl.BlockSpec((tm, n_heads * LANES), lambda i: (i, 0)),
        ],
        out_shape=[
            jax.ShapeDtypeStruct((M, n_heads * QK_PAD), BF16),
            jax.ShapeDtypeStruct((M, n_heads * LANES), BF16),
        ],
        compiler_params=_cparams(("parallel",), 56),
        name="mla_kv",
    )(h, w_ckv, kv_norm_g.reshape(1, R), w_k, w_v, w_ra, w_rb, cos_t, sin_t)


def _fox_prep_kernel(n_heads, inv_scale, h_ref, wf_ref, bf_ref, q_ref, k_ref,
                     qo_ref, ko_ref, carry_ref):
    @pl.when(pl.program_id(1) == 0)
    def _():
        carry_ref[...] = jnp.zeros_like(carry_ref)

    f = jnp.dot(h_ref[...], wf_ref[...], preferred_element_type=F32) + bf_ref[...]
    log_f = jax.nn.log_sigmoid(f)
    ts = log_f.shape[0]
    row = lax.broadcasted_iota(jnp.int32, (ts, ts), 0)
    col = lax.broadcasted_iota(jnp.int32, (ts, ts), 1)
    tri = jnp.where(col <= row, 1.0, 0.0).astype(BF16)
    cum = carry_ref[0:1, :]
    for part in _split3(log_f):
        cum = cum + jnp.dot(tri, part, preferred_element_type=F32)
    carry_ref[...] = jnp.broadcast_to(cum[ts - 1:ts, :], carry_ref.shape)

    lane = lax.broadcasted_iota(jnp.int32, (ts, LANES), 1)
    cum_s = cum * inv_scale
    for hd in range(n_heads):
        colv = jnp.broadcast_to(cum_s[:, hd:hd + 1], (ts, LANES))
        hi, mid, lo = (part.astype(F32) for part in _split3(colv))
        q_aug = jnp.where(lane == 0, hi, jnp.where(lane == 1, mid, jnp.where(
            lane == 2, lo, jnp.where(lane < 6, 1.0, 0.0))))
        k_aug = jnp.where(lane < 3, 1.0, jnp.where(lane == 3, -hi, jnp.where(
            lane == 4, -mid, jnp.where(lane == 5, -lo, 0.0))))
        o = hd * QK_PAD
        qo_ref[:, o:o + LANES] = q_ref[:, hd * LANES:(hd + 1) * LANES]
        qo_ref[:, o + LANES:o + QK_PAD] = q_aug.astype(BF16)
        ko_ref[:, o:o + LANES] = k_ref[:, hd * LANES:(hd + 1) * LANES]
        ko_ref[:, o + LANES:o + QK_PAD] = k_aug.astype(BF16)


def _fox_prep_call(h, u, w_f, b_f, B, S, n_heads, q_blk, k_blk, scale):
    M, D = h.shape
    W = n_heads * LANES
    ts = _tile(S, 512)
    nb = S // ts
    row = lambda b, i: (b * nb + i, 0)
    return pl.pallas_call(
        functools.partial(_fox_prep_kernel, n_heads, 1.0 / scale),
        grid=(B, nb),
        in_specs=[
            pl.BlockSpec((ts, D), row),
            pl.BlockSpec((D, LANES), lambda b, i: (0, 0)),
            pl.BlockSpec((1, LANES), lambda b, i: (0, 0)),
            pl.BlockSpec((ts, W), lambda b, i: (b * nb + i, q_blk)),
            pl.BlockSpec((ts, W), lambda b, i: (b * nb + i, k_blk)),
        ],
        out_specs=[
            pl.BlockSpec((ts, n_heads * QK_PAD), row),
            pl.BlockSpec((ts, n_heads * QK_PAD), row),
        ],
        out_shape=[
            jax.ShapeDtypeStruct((M, n_heads * QK_PAD), BF16),
            jax.ShapeDtypeStruct((M, n_heads * QK_PAD), BF16),
        ],
        scratch_shapes=[pltpu.VMEM((8, LANES), F32)],
        compiler_params=_cparams(("parallel", "arbitrary")),
        name="fox_prep",
    )(h, w_f, b_f, u, u)


def _flash_kernel(scale, qi_ref, kj_ref, q_ref, k_ref, v_ref, o_ref, m_ref, l_ref, acc_ref):
    p_id = pl.program_id(2)
    i = qi_ref[p_id]
    j = kj_ref[p_id]

    @pl.when(j == 0)
    def _():
        m_ref[...] = jnp.full_like(m_ref, -jnp.inf)
        l_ref[...] = jnp.zeros_like(l_ref)
        acc_ref[...] = jnp.zeros_like(acc_ref)

    def step(masked):
        s = lax.dot_general(q_ref[...], k_ref[...], (((1,), (1,)), ((), ())),
                            preferred_element_type=F32) * scale
        if masked:
            row = lax.broadcasted_iota(jnp.int32, s.shape, 0)
            col = lax.broadcasted_iota(jnp.int32, s.shape, 1)
            s = jnp.where(col <= row, s, -jnp.inf)
        m_prev = m_ref[...]
        m_new = jnp.maximum(m_prev, jnp.max(s, axis=-1, keepdims=True))
        alpha = jnp.exp(m_prev - m_new)
        p = jnp.exp(s - m_new)
        l_ref[...] = alpha * l_ref[...] + jnp.sum(p, axis=-1, keepdims=True)
        acc_ref[...] = alpha * acc_ref[...] + jnp.dot(p.astype(v_ref.dtype), v_ref[...],
                                                      preferred_element_type=F32)
        m_ref[...] = m_new

    @pl.when(j < i)
    def _():
        step(False)

    @pl.when(j == i)
    def _():
        step(True)
        o_ref[...] = (acc_ref[...] / l_ref[...]).astype(o_ref.dtype)


def _flash_call(q, k, v, B, S, n_heads, scale, v_blk0=0, name="flash"):
    M = q.shape[0]
    t = _tile(S, 512)
    nq = S // t
    pairs = [(i, j) for i in range(nq) for j in range(i + 1)]
    qi = jnp.asarray([p[0] for p in pairs], jnp.int32)
    kj = jnp.asarray([p[1] for p in pairs], jnp.int32)
    grid_spec = pltpu.PrefetchScalarGridSpec(
        num_scalar_prefetch=2,
        grid=(B, n_heads, len(pairs)),
        in_specs=[
            pl.BlockSpec((t, QK_PAD), lambda b, h, p, qi, kj: (b * nq + qi[p], h)),
            pl.BlockSpec((t, QK_PAD), lambda b, h, p, qi, kj: (b * nq + kj[p], h)),
            pl.BlockSpec((t, HEAD_DIM), lambda b, h, p, qi, kj: (b * nq + kj[p], v_blk0 + h)),
        ],
        out_specs=pl.BlockSpec((t, HEAD_DIM), lambda b, h, p, qi, kj: (b * nq + qi[p], h)),
        scratch_shapes=[
            pltpu.VMEM((t, 1), F32),
            pltpu.VMEM((t, 1), F32),
            pltpu.VMEM((t, HEAD_DIM), F32),
        ],
    )
    return pl.pallas_call(
        functools.partial(_flash_kernel, scale),
        grid_spec=grid_spec,
        out_shape=jax.ShapeDtypeStruct((M, n_heads * HEAD_DIM), BF16),
        compiler_params=_cparams(("parallel", "parallel", "arbitrary")),
        name=name,
    )(qi, kj, q, k, v)


def _merge_kernel(n_br, h_ref, wg_ref, bg_ref, *rest):
    br_refs = rest[:n_br]
    wb_ref = rest[n_br]
    o_ref = rest[n_br + 1]
    hb = h_ref[...]
    merged = None
    for n in range(n_br):
        gate = jax.nn.sigmoid(jnp.dot(hb, wg_ref[n], preferred_element_type=F32) + bg_ref[n])
        proj = jnp.dot(br_refs[n][...], wb_ref[n], preferred_element_type=F32)
        merged = gate * proj if merged is None else merged + gate * proj
    o_ref[...] = merged.astype(o_ref.dtype)


def _merge_call(h, w_gate, b_gate, branches, w_branch):
    M, D = h.shape
    n_br, W, _ = w_branch.shape
    tm = _tile(M, 512)
    tn = _tile(D, 256)
    in_specs = [
        pl.BlockSpec((tm, D), lambda i, j: (i, 0)),
        pl.BlockSpec((n_br, D, tn), lambda i, j: (0, 0, j)),
        pl.BlockSpec((n_br, 1, tn), lambda i, j: (0, 0, j)),
    ]
    in_specs += [pl.BlockSpec((tm, W), lambda i, j: (i, 0)) for _ in range(n_br)]
    in_specs += [pl.BlockSpec((n_br, W, tn), lambda i, j: (0, 0, j))]
    return pl.pallas_call(
        functools.partial(_merge_kernel, n_br),
        grid=(M // tm, D // tn),
        in_specs=in_specs,
        out_specs=pl.BlockSpec((tm, tn), lambda i, j: (i, j)),
        out_shape=jax.ShapeDtypeStruct((M, D), BF16),
        compiler_params=_cparams(("parallel", "parallel"), 56),
        name="gated_merge",
    )(h, w_gate, b_gate.reshape(n_br, 1, D), *branches, w_branch)


def _post_kernel(x_ref, y_ref, g_ref, gate_ref, o_ref):
    o_ref[0] = x_ref[0] + gate_ref[0] * _rms(y_ref[0], g_ref[...])


def _post_call(x, y, g, gate):
    B, S, D = x.shape
    ts = _tile(S, 256)
    blk = pl.BlockSpec((1, ts, D), lambda b, i: (b, i, 0))
    return pl.pallas_call(
        _post_kernel,
        grid=(B, S // ts),
        in_specs=[blk, blk, pl.BlockSpec((1, D), lambda b, i: (0, 0)),
                  pl.BlockSpec((1, 1, D), lambda b, i: (b, 0, 0))],
        out_specs=blk,
        out_shape=jax.ShapeDtypeStruct((B, S, D), F32),
        compiler_params=_cparams(("parallel", "parallel")),
        name="post_residual",
    )(x, y.reshape(B, S, D), g.reshape(1, D), gate)


def _combine_post_kernel(x_ref, y_ref, p_ref, g_ref, gate_ref, o_ref):
    p = p_ref[...]
    y = y_ref[0] * p[:, 0:1]
    for k in range(1, TOP_K):
        y = y + y_ref[k] * p[:, k:k + 1]
    o_ref[0] = x_ref[0] + gate_ref[0] * _rms(y, g_ref[...])


def _combine_post_call(x, y_tok, top_p, g, gate):
    B, S, D = x.shape
    ts = _tile(S, 128)
    nb = S // ts
    blk = pl.BlockSpec((1, ts, D), lambda b, i: (b, i, 0))
    return pl.pallas_call(
        _combine_post_kernel,
        grid=(B, nb),
        in_specs=[blk,
                  pl.BlockSpec((TOP_K, ts, D), lambda b, i: (0, b * nb + i, 0)),
                  pl.BlockSpec((ts, LANES), lambda b, i: (b * nb + i, 0)),
                  pl.BlockSpec((1, D), lambda b, i: (0, 0)),
                  pl.BlockSpec((1, 1, D), lambda b, i: (b, 0, 0))],
        out_specs=blk,
        out_shape=jax.ShapeDtypeStruct((B, S, D), F32),
        compiler_params=_cparams(("parallel", "parallel")),
        name="combine_post",
    )(x, y_tok, top_p, g.reshape(1, D), gate)


def _rank_kernel(e_ref, rank_ref, cnt_ref, carry_ref):
    @pl.when(pl.program_id(0) == 0)
    def _():
        carry_ref[...] = jnp.zeros_like(carry_ref)

    e = e_ref[...]
    tb = e.shape[0]
    lane = lax.broadcasted_iota(jnp.int32, e.shape, 1)
    sel = [jnp.broadcast_to(e[:, k:k + 1], e.shape) == lane for k in range(TOP_K)]
    onehot = jnp.zeros(e.shape, F32)
    for s in sel:
        onehot = onehot + jnp.where(s, 1.0, 0.0)
    row = lax.broadcasted_iota(jnp.int32, (tb, tb), 0)
    col = lax.broadcasted_iota(jnp.int32, (tb, tb), 1)
    strict = jnp.where(col < row, 1.0, 0.0).astype(BF16)
    before = carry_ref[0:1, :] + jnp.dot(strict, onehot.astype(BF16), preferred_element_type=F32)
    rank = jnp.zeros(e.shape, F32)
    for k in range(TOP_K):
        r_k = jnp.sum(jnp.where(sel[k], before, 0.0), axis=-1, keepdims=True)
        rank = jnp.where(lane == k, r_k, rank)
    rank_ref[...] = rank.astype(jnp.int32)
    total = carry_ref[0:1, :] + jnp.sum(onehot, axis=0, keepdims=True)
    carry_ref[...] = jnp.broadcast_to(total, carry_ref.shape)
    cnt_ref[...] = jnp.broadcast_to(total, cnt_ref.shape).astype(jnp.int32)


def _rank_call(top_e):
    N = top_e.shape[0]
    tb = _tile(N, 512)
    return pl.pallas_call(
        _rank_kernel,
        grid=(N // tb,),
        in_specs=[pl.BlockSpec((tb, LANES), lambda i: (i, 0))],
        out_specs=[pl.BlockSpec((tb, LANES), lambda i: (i, 0)),
                   pl.BlockSpec((8, LANES), lambda i: (0, 0))],
        out_shape=[jax.ShapeDtypeStruct((N, LANES), jnp.int32),
                   jax.ShapeDtypeStruct((8, LANES), jnp.int32)],
        scratch_shapes=[pltpu.VMEM((8, LANES), F32)],
        compiler_params=_cparams(("arbitrary",)),
        name="route_rank",
    )(top_e)


def _row_copy(src, dst, s_row, d_row, sem):
    return pltpu.make_async_copy(src.at[pl.ds(s_row, 1)], dst.at[pl.ds(d_row, 1)], sem)


def _dispatch_kernel(n_tok, n_experts, n_blocks, dest_ref, zlo_ref, zhi_ref, nblk_ref,
                     h_ref, xg_ref, zero_ref, sem):
    zero_ref[...] = jnp.zeros_like(zero_ref)

    def tok_body(t, c):
        for k in range(TOP_K):
            _row_copy(h_ref, xg_ref, t, dest_ref[t * TOP_K + k], sem).start()
        return c

    lax.fori_loop(0, n_tok, tok_body, 0)

    def exp_body(e, c):
        def row_body(r, c2):
            _row_copy(zero_ref, xg_ref, 0, r, sem).start()
            return c2
        return lax.fori_loop(zlo_ref[e], zhi_ref[e], row_body, c)

    lax.fori_loop(0, n_experts, exp_body, 0)

    def block_copy(b):
        return pltpu.make_async_copy(zero_ref, xg_ref.at[pl.ds(b * EXPERT_ROWS, EXPERT_ROWS)], sem)

    def tail_body(b, c):
        block_copy(b).start()
        return c

    lax.fori_loop(nblk_ref[0], n_blocks, tail_body, 0)

    def wait_body(b, c):
        block_copy(b).wait()
        return c

    lax.fori_loop(0, n_blocks, wait_body, 0)


def _dispatch_call(h, dest_flat, zlo, zhi, n_used, n_rows):
    N, D = h.shape
    E = zlo.shape[0]
    grid_spec = pltpu.PrefetchScalarGridSpec(
        num_scalar_prefetch=4,
        grid=(1,),
        in_specs=[pl.BlockSpec(memory_space=pl.ANY)],
        out_specs=pl.BlockSpec(memory_space=pl.ANY),
        scratch_shapes=[pltpu.VMEM((EXPERT_ROWS, D), F32), pltpu.SemaphoreType.DMA(())],
    )
    return pl.pallas_call(
        functools.partial(_dispatch_kernel, N, E, n_rows // EXPERT_ROWS),
        grid_spec=grid_spec,
        out_shape=jax.ShapeDtypeStruct((n_rows, D), F32),
        compiler_params=pltpu.CompilerParams(dimension_semantics=("arbitrary",),
                                             has_side_effects=True),
        name="moe_dispatch",
    )(dest_flat, zlo, zhi, n_used, h)


def _gather_kernel(n_tok, rows_per_wait, dest_ref, y_ref, o_ref, sem):
    def tok_body(t, c):
        for k in range(TOP_K):
            _row_copy(y_ref, o_ref, dest_ref[t * TOP_K + k], k * n_tok + t, sem).start()
        return c

    lax.fori_loop(0, n_tok, tok_body, 0)

    def wait_body(b, c):
        pltpu.make_async_copy(o_ref.at[pl.ds(0, rows_per_wait)],
                              o_ref.at[pl.ds(0, rows_per_wait)], sem).wait()
        return c

    lax.fori_loop(0, (n_tok * TOP_K) // rows_per_wait, wait_body, 0)


def _gather_call(y_grouped, dest_flat, n_tok):
    D = y_grouped.shape[1]
    rows_per_wait = _tile(n_tok * TOP_K, 256)
    grid_spec = pltpu.PrefetchScalarGridSpec(
        num_scalar_prefetch=1,
        grid=(1,),
        in_specs=[pl.BlockSpec(memory_space=pl.ANY)],
        out_specs=pl.BlockSpec(memory_space=pl.ANY),
        scratch_shapes=[pltpu.SemaphoreType.DMA(())],
    )
    return pl.pallas_call(
        functools.partial(_gather_kernel, n_tok, rows_per_wait),
        grid_spec=grid_spec,
        out_shape=jax.ShapeDtypeStruct((TOP_K * n_tok, D), F32),
        compiler_params=pltpu.CompilerParams(dimension_semantics=("arbitrary",),
                                             has_side_effects=True),
        name="moe_gather",
    )(dest_flat, y_grouped)


def _expert_kernel(f_dim, be_ref, nu_ref, x_ref, wgu_ref, bgu_ref, wd_ref, bd_ref, o_ref):
    i = pl.program_id(0)

    @pl.when(i < nu_ref[0])
    def _():
        xb = x_ref[...].astype(BF16)
        gu = jnp.dot(xb, wgu_ref[0], preferred_element_type=F32) + bgu_ref[0]
        g = jnp.minimum(gu[:, :f_dim], SWIGLU_LIMIT)
        up = jnp.clip(gu[:, f_dim:], -SWIGLU_LIMIT, SWIGLU_LIMIT)
        act = g * jax.nn.sigmoid(SWIGLU_ALPHA * g) * (up + 1.0)
        o_ref[...] = jnp.dot(act.astype(BF16), wd_ref[0], preferred_element_type=F32) + bd_ref[0]

    @pl.when(i >= nu_ref[0])
    def _():
        o_ref[...] = jnp.zeros_like(o_ref)


def _expert_call(x_grouped, block_e, n_used, w_gu, b_gu, w_d, b_d):
    n_rows, D = x_grouped.shape
    E, _, F2 = w_gu.shape
    F = F2 // 2
    n_blocks = n_rows // EXPERT_ROWS
    grid_spec = pltpu.PrefetchScalarGridSpec(
        num_scalar_prefetch=2,
        grid=(n_blocks,),
        in_specs=[
            pl.BlockSpec((EXPERT_ROWS, D), lambda i, be, nu: (jnp.minimum(i, nu[0] - 1), 0)),
            pl.BlockSpec((1, D, F2), lambda i, be, nu: (be[i], 0, 0)),
            pl.BlockSpec((1, 1, F2), lambda i, be, nu: (be[i], 0, 0)),
            pl.BlockSpec((1, F, D), lambda i, be, nu: (be[i], 0, 0)),
            pl.BlockSpec((1, 1, D), lambda i, be, nu: (be[i], 0, 0)),
        ],
        out_specs=pl.BlockSpec((EXPERT_ROWS, D), lambda i, be, nu: (i, 0)),
    )
    return pl.pallas_call(
        functools.partial(_expert_kernel, F),
        grid_spec=grid_spec,
        out_shape=jax.ShapeDtypeStruct((n_rows, D), F32),
        compiler_params=_cparams(("arbitrary",), 56),
        name="experts",
    )(block_e, n_used, x_grouped, w_gu, b_gu.reshape(E, 1, F2), w_d, b_d.reshape(E, 1, D))


def _mixer_weights(D, w_mix_in, w_uq, w_ukv, fox_f_bias):
    Wb = D // 4
    H = Wb // HEAD_DIM
    q_lora = 3 * D // 16
    kv_lora = D // 8
    half = ROPE_DIM // 2
    o_cq = 3 * Wb
    o_ckv = o_cq + q_lora
    o_kr = o_ckv + kv_lora
    o_fox = o_kr + ROPE_DIM
    o_f = o_fox + 3 * Wb
    w_convfox = jnp.concatenate([w_mix_in[:, :o_cq], w_mix_in[:, o_fox:o_f]], axis=1).astype(BF16)
    w_cq = w_mix_in[:, o_cq:o_ckv].astype(BF16)
    w_ckv = w_mix_in[:, o_ckv:o_kr].astype(BF16)
    w_kr = w_mix_in[:, o_kr:o_fox]
    zpad = jnp.zeros((D, LANES - ROPE_DIM), F32)
    w_kra = jnp.concatenate([w_kr, zpad], axis=1).astype(BF16)
    w_krb = jnp.concatenate([-w_kr[:, half:], w_kr[:, :half], zpad], axis=1).astype(BF16)
    w_f = jnp.pad(w_mix_in[:, o_f:o_f + H], ((0, 0), (0, LANES - H))).astype(BF16)
    b_f = jnp.pad(fox_f_bias, (0, LANES - H)).reshape(1, LANES)

    uq = w_uq.reshape(q_lora, H, HEAD_DIM + ROPE_DIM)
    zq = jnp.zeros((q_lora, H, QK_PAD - HEAD_DIM - ROPE_DIM), F32)
    w_qa = jnp.concatenate([uq, zq], axis=2).reshape(q_lora, H * QK_PAD).astype(BF16)
    r1 = uq[:, :, HEAD_DIM:HEAD_DIM + half]
    r2 = uq[:, :, HEAD_DIM + half:]
    w_qb = jnp.concatenate([-r2, r1, zq], axis=2).reshape(q_lora, H * LANES).astype(BF16)

    ukv = w_ukv.reshape(kv_lora, H, 2 * HEAD_DIM)
    w_k = ukv[:, :, :HEAD_DIM].reshape(kv_lora, H * HEAD_DIM).astype(BF16)
    w_v = ukv[:, :, HEAD_DIM:].reshape(kv_lora, H * HEAD_DIM).astype(BF16)
    return dict(w_convfox=w_convfox, w_cq=w_cq, w_ckv=w_ckv, w_kra=w_kra, w_krb=w_krb,
                w_f=w_f, b_f=b_f, w_qa=w_qa, w_qb=w_qb, w_k=w_k, w_v=w_v)


def _mixer(h, B, S, D, rope_cos, rope_sin, mw, conv_w, q_norm_g, kv_norm_g,
           w_branch, w_merge_gate, b_merge_gate, w_o):
    Wb = D // 4
    H = Wb // HEAD_DIM
    u = _mm(h, mw["w_convfox"], BF16, tn_pref=512, name="mix_in")
    o_a = _conv_call(u, conv_w, B, S, Wb)

    q_mla = _mla_q_call(h, mw["w_cq"], q_norm_g, mw["w_qa"], mw["w_qb"], rope_cos, rope_sin, H)
    k_mla, v_mla = _mla_kv_call(h, mw["w_ckv"], kv_norm_g, mw["w_k"], mw["w_v"],
                                mw["w_kra"], mw["w_krb"], rope_cos, rope_sin, H)
    o_b = _flash_call(q_mla, k_mla, v_mla, B, S, H, (HEAD_DIM + ROPE_DIM) ** -0.5, name="flash_mla")

    fox_scale = HEAD_DIM ** -0.5
    q_fox, k_fox = _fox_prep_call(h, u, mw["w_f"], mw["b_f"], B, S, H, 3, 4, fox_scale)
    o_c = _flash_call(q_fox, k_fox, u, B, S, H, fox_scale, v_blk0=5 * H, name="flash_fox")

    merged = _merge_call(h, w_merge_gate.astype(BF16), b_merge_gate, (o_a, o_b, o_c),
                         w_branch.astype(BF16))
    return _mm(merged, w_o.astype(BF16), F32, name="mix_out")


def _moe(x, g_pre, scale, shift, g_post, gate, w_router, b_router, w_gate_up, b_gate_up,
         w_down, b_down):
    B, S, D = x.shape
    N = B * S
    E = w_router.shape[1]
    h, top_e, top_p = _norm_router_call(x, g_pre, scale, shift, w_router, b_router)
    rank, counts8 = _rank_call(top_e)

    counts = counts8[0, :E]
    padded = (counts + EXPERT_ROWS - 1) // EXPERT_ROWS * EXPERT_ROWS
    pad_end = jnp.cumsum(padded)
    pad_start = pad_end - padded
    e_flat = top_e[:, :TOP_K].reshape(-1)
    dest = pad_start[e_flat] + rank[:, :TOP_K].reshape(-1)
    n_blocks = (N * TOP_K + E * (EXPERT_ROWS - 1) + EXPERT_ROWS - 1) // EXPERT_ROWS
    n_rows = n_blocks * EXPERT_ROWS
    block_e = jnp.minimum(
        jnp.searchsorted(pad_end, jnp.arange(n_blocks, dtype=jnp.int32) * EXPERT_ROWS, side="right"),
        E - 1).astype(jnp.int32)
    n_used = (pad_end[-1:] // EXPERT_ROWS).astype(jnp.int32)

    x_grouped = _dispatch_call(h.reshape(N, D), dest.astype(jnp.int32),
                               (pad_start + counts).astype(jnp.int32),
                               pad_end.astype(jnp.int32), n_used, n_rows)
    y_grouped = _expert_call(x_grouped, block_e, n_used, w_gate_up.astype(BF16), b_gate_up,
                             w_down.astype(BF16), b_down)
    y_tok = _gather_call(y_grouped, dest.astype(jnp.int32), N).reshape(TOP_K, N, D)
    return _combine_post_call(x, y_tok, top_p, g_post, gate)


def kernel(x, c, positions, w_ada, b_ada, g_mix_pre, g_mix_post, g_ffn_pre, g_ffn_post, w_mix_in, conv_w, q_norm_g, kv_norm_g, w_uq, w_ukv, fox_f_bias, w_branch, w_merge_gate, b_merge_gate, w_o, w_router, b_router, w_gate_up, b_gate_up, w_down, b_down):
    B, S, D = x.shape
    L = w_ada.shape[0]
    N = B * S

    c8 = jnp.pad(c, ((0, 8 - B), (0, 0)))
    ada = _ada(c8, w_ada, b_ada)[:, :B].reshape(L, B, N_ADA, 1, D)

    half = ROPE_DIM // 2
    inv_freq = ROPE_THETA ** (-jnp.arange(half, dtype=F32) / half)
    ang = positions.astype(F32).reshape(N, 1) * inv_freq
    zer = jnp.zeros((N, LANES - ROPE_DIM), F32)
    rope_cos = jnp.concatenate([jnp.cos(ang), jnp.cos(ang), zer], axis=1)
    rope_sin = jnp.concatenate([jnp.sin(ang), jnp.sin(ang), zer], axis=1)

    for l in range(L):
        shift_m, scale_m, gate_m = ada[l, :, 0], ada[l, :, 1], ada[l, :, 2]
        shift_f, scale_f, gate_f = ada[l, :, 3], ada[l, :, 4], ada[l, :, 5]

        h = _norm_mod_call(x, g_mix_pre[l], scale_m, shift_m).reshape(N, D)
        mw = _mixer_weights(D, w_mix_in[l], w_uq[l], w_ukv[l], fox_f_bias[l])
        y = _mixer(h, B, S, D, rope_cos, rope_sin, mw, conv_w[l], q_norm_g[l], kv_norm_g[l],
                   w_branch[l], w_merge_gate[l], b_merge_gate[l], w_o[l])
        x = _post_call(x, y, g_mix_post[l], gate_m)

        x = _moe(x, g_ffn_pre[l], scale_f, shift_f, g_ffn_post[l], gate_f, w_router[l],
                 b_router[l], w_gate_up[l], b_gate_up[l], w_down[l], b_down[l])
    return x
```

```python
import functools

import jax
import jax.numpy as jnp
from jax import lax
from jax.experimental import pallas as pl
from jax.experimental.pallas import tpu as pltpu

NORM_EPS = 1e-6
ROPE_THETA = 10000.0
HEAD_DIM = 128
ROPE_DIM = 64
CONV_K = 3
TOP_K = 4
SWIGLU_ALPHA = 1.702
SWIGLU_LIMIT = 7.0
N_ADA = 6
LANES = 128
QK_PAD = 2 * LANES
EXPERT_ROWS = 256
MIB = 1024 * 1024

F32 = jnp.float32
BF16 = jnp.bfloat16


def _cparams(sem, vmem_mib=48):
    return pltpu.CompilerParams(dimension_semantics=sem, vmem_limit_bytes=vmem_mib * MIB)


def _tile(n, pref):
    t = min(pref, n)
    while n % t:
        t //= 2
    return t


def _store_token_major(ref, val):
    rows, d = val.shape
    ch = d // LANES
    for j in range(ch):
        ref[pl.ds(j, rows, stride=ch), :] = val[:, j * LANES:(j + 1) * LANES]


def _load_token_major(ref, rows, lead=None):
    ch = ref.shape[-2] // rows
    parts = []
    for j in range(ch):
        idx = (pl.ds(j, rows, stride=ch), slice(None))
        parts.append(ref[idx] if lead is None else ref[(lead,) + idx])
    return jnp.concatenate(parts, axis=1)


def _split3(x):
    hi = x.astype(BF16)
    r1 = x - hi.astype(F32)
    mid = r1.astype(BF16)
    lo = (r1 - mid.astype(F32)).astype(BF16)
    return hi, mid, lo


def _ada_kernel(c_ref, w_ref, b_ref, o_ref):
    c = c_ref[...]
    c_act = (c * jax.nn.sigmoid(c)).astype(BF16)
    acc = jnp.dot(c_act, w_ref[0].astype(BF16), preferred_element_type=F32)
    o_ref[0] = acc + b_ref[0]


def _ada(c8, w_ada, b_ada):
    L, D, N = w_ada.shape
    tn = _tile(N, 1024)
    return pl.pallas_call(
        _ada_kernel,
        grid=(L, N // tn),
        in_specs=[
            pl.BlockSpec((8, D), lambda l, j: (0, 0)),
            pl.BlockSpec((1, D, tn), lambda l, j: (l, 0, j)),
            pl.BlockSpec((1, 1, tn), lambda l, j: (l, 0, j)),
        ],
        out_specs=pl.BlockSpec((1, 8, tn), lambda l, j: (l, 0, j)),
        out_shape=jax.ShapeDtypeStruct((L, 8, N), F32),
        compiler_params=_cparams(("parallel", "parallel"), 56),
        name="ada",
    )(c8, w_ada, b_ada.reshape(L, 1, N))


def _norm_mod(x, g, scale, shift):
    xf = x
    y = xf * lax.rsqrt(jnp.mean(xf * xf, axis=-1, keepdims=True) + NORM_EPS)
    return (y * g) * (1.0 + scale) + shift


def _norm_mod_kernel(x_ref, g_ref, sc_ref, sh_ref, o_ref):
    o_ref[0] = _norm_mod(x_ref[0], g_ref[...], sc_ref[0], sh_ref[0]).astype(o_ref.dtype)


def _norm_mod_call(x, g, scale, shift):
    B, S, D = x.shape
    ts = _tile(S, 256)
    return pl.pallas_call(
        _norm_mod_kernel,
        grid=(B, S // ts),
        in_specs=[
            pl.BlockSpec((1, ts, D), lambda b, i: (b, i, 0)),
            pl.BlockSpec((1, D), lambda b, i: (0, 0)),
            pl.BlockSpec((1, 1, D), lambda b, i: (b, 0, 0)),
            pl.BlockSpec((1, 1, D), lambda b, i: (b, 0, 0)),
        ],
        out_specs=pl.BlockSpec((1, ts, D), lambda b, i: (b, i, 0)),
        out_shape=jax.ShapeDtypeStruct((B, S, D), BF16),
        compiler_params=_cparams(("parallel", "parallel")),
        name="norm_mod",
    )(x, g.reshape(1, D), scale, shift)


def _norm_router_kernel(n_experts, x_ref, g_ref, sc_ref, sh_ref, wr_hi_ref, wr_lo_ref, br_ref,
                        h_ref, e_ref, p_ref):
    h = _norm_mod(x_ref[0], g_ref[...], sc_ref[0], sh_ref[0])
    _store_token_major(h_ref, h)
    h_hi = h.astype(BF16)
    h_lo = (h - h_hi.astype(F32)).astype(BF16)
    logits = (jnp.dot(h_hi, wr_hi_ref[...], preferred_element_type=F32)
              + jnp.dot(h_hi, wr_lo_ref[...], preferred_element_type=F32)
              + jnp.dot(h_lo, wr_hi_ref[...], preferred_element_type=F32)) + br_ref[...]
    lane = lax.broadcasted_iota(jnp.int32, logits.shape, 1)
    vals = jnp.where(lane < n_experts, logits, -jnp.inf)
    tops, idxs = [], []
    for _ in range(TOP_K):
        m = jnp.max(vals, axis=-1, keepdims=True)
        idx = jnp.min(jnp.where(vals == m, lane, LANES), axis=-1, keepdims=True)
        tops.append(m)
        idxs.append(idx)
        vals = jnp.where(lane == idx, -jnp.inf, vals)
    exps = [jnp.exp(t - tops[0]) for t in tops]
    denom = exps[0]
    for e in exps[1:]:
        denom = denom + e
    e_out = jnp.zeros(logits.shape, jnp.int32)
    p_out = jnp.zeros(logits.shape, F32)
    for k in range(TOP_K):
        e_out = jnp.where(lane == k, idxs[k], e_out)
        p_out = jnp.where(lane == k, exps[k] / denom, p_out)
    e_ref[...] = e_out
    p_ref[...] = p_out


def _norm_router_call(x, g, scale, shift, w_router, b_router):
    B, S, D = x.shape
    E = w_router.shape[1]
    ts = _tile(S, 256)
    nb = S // ts
    wr = jnp.pad(w_router, ((0, 0), (0, LANES - E)))
    wr_hi = wr.astype(BF16)
    wr_lo = (wr - wr_hi.astype(F32)).astype(BF16)
    br = jnp.pad(b_router, (0, LANES - E)).reshape(1, LANES)
    return pl.pallas_call(
        functools.partial(_norm_router_kernel, E),
        grid=(B, nb),
        in_specs=[
            pl.BlockSpec((1, ts, D), lambda b, i: (b, i, 0)),
            pl.BlockSpec((1, D), lambda b, i: (0, 0)),
            pl.BlockSpec((1, 1, D), lambda b, i: (b, 0, 0)),
            pl.BlockSpec((1, 1, D), lambda b, i: (b, 0, 0)),
            pl.BlockSpec((D, LANES), lambda b, i: (0, 0)),
            pl.BlockSpec((D, LANES), lambda b, i: (0, 0)),
            pl.BlockSpec((1, LANES), lambda b, i: (0, 0)),
        ],
        out_specs=[
            pl.BlockSpec((ts * (D // LANES), LANES), lambda b, i: (b * nb + i, 0)),
            pl.BlockSpec((ts, LANES), lambda b, i: (b * nb + i, 0)),
            pl.BlockSpec((ts, LANES), lambda b, i: (b * nb + i, 0)),
        ],
        out_shape=[
            jax.ShapeDtypeStruct((B * S * (D // LANES), LANES), F32),
            jax.ShapeDtypeStruct((B * S, LANES), jnp.int32),
            jax.ShapeDtypeStruct((B * S, LANES), F32),
        ],
        compiler_params=_cparams(("parallel", "parallel")),
        name="norm_router",
    )(x, g.reshape(1, D), scale, shift, wr_hi, wr_lo, br)


def _mm_kernel(a_ref, w_ref, o_ref):
    o_ref[...] = jnp.dot(a_ref[...], w_ref[...], preferred_element_type=F32).astype(o_ref.dtype)


def _mm(a, w, out_dtype, tm_pref=1024, tn_pref=512, name="mm"):
    M, K = a.shape
    N = w.shape[1]
    tm = _tile(M, tm_pref)
    tn = _tile(N, tn_pref)
    return pl.pallas_call(
        _mm_kernel,
        grid=(M // tm, N // tn),
        in_specs=[
            pl.BlockSpec((tm, K), lambda i, j: (i, 0)),
            pl.BlockSpec((K, tn), lambda i, j: (0, j)),
        ],
        out_specs=pl.BlockSpec((tm, tn), lambda i, j: (i, j)),
        out_shape=jax.ShapeDtypeStruct((M, N), out_dtype),
        compiler_params=_cparams(("parallel", "parallel"), 56),
        name=name,
    )(a, w)


def _conv_kernel(b_ref, c_ref, h_ref, w_ref, o_ref):
    g = c_ref[...].astype(F32) * h_ref[...].astype(F32)
    row = lax.broadcasted_iota(jnp.int32, g.shape, 0)
    g1 = jnp.where(row >= 1, pltpu.roll(g, 1, 0), 0.0)
    g2 = jnp.where(row >= 2, pltpu.roll(g, 2, 0), 0.0)
    w = w_ref[...]
    conv = w[0:1] * g2 + w[1:2] * g1 + w[2:3] * g
    o_ref[...] = (b_ref[...].astype(F32) * conv).astype(o_ref.dtype)


def _conv_call(u, conv_w, B, S, C):
    tc = _tile(C, 256)
    nc = C // tc
    w8 = jnp.pad(conv_w, ((0, 8 - CONV_K), (0, 0)))
    return pl.pallas_call(
        _conv_kernel,
        grid=(B, nc),
        in_specs=[
            pl.BlockSpec((S, tc), lambda b, j: (b, j)),
            pl.BlockSpec((S, tc), lambda b, j: (b, nc + j)),
            pl.BlockSpec((S, tc), lambda b, j: (b, 2 * nc + j)),
            pl.BlockSpec((8, tc), lambda b, j: (0, j)),
        ],
        out_specs=pl.BlockSpec((S, tc), lambda b, j: (b, j)),
        out_shape=jax.ShapeDtypeStruct((B * S, C), BF16),
        compiler_params=_cparams(("parallel", "parallel")),
        name="gated_conv",
    )(u, u, u, w8)


def _rms(x, g):
    return x * lax.rsqrt(jnp.mean(x * x, axis=-1, keepdims=True) + NORM_EPS) * g


def _mla_q_kernel(n_heads, h_ref, wc_ref, g_ref, wa_ref, wb_ref, cos_ref, sin_ref, q_ref):
    c_q = jnp.dot(h_ref[...], wc_ref[...], preferred_element_type=F32)
    n = _rms(c_q, g_ref[...]).astype(BF16)
    a = jnp.dot(n, wa_ref[...], preferred_element_type=F32)
    bm = jnp.dot(n, wb_ref[...], preferred_element_type=F32)
    cos = cos_ref[...]
    sin = sin_ref[...]
    for hd in range(n_heads):
        o = hd * QK_PAD
        q_ref[:, o:o + LANES] = a[:, o:o + LANES].astype(q_ref.dtype)
        rot = a[:, o + LANES:o + QK_PAD] * cos + bm[:, hd * LANES:(hd + 1) * LANES] * sin
        q_ref[:, o + LANES:o + QK_PAD] = rot.astype(q_ref.dtype)


def _mla_q_call(h, w_cq, q_norm_g, w_a, w_b, cos_t, sin_t, n_heads):
    M, D = h.shape
    R = w_cq.shape[1]
    tm = _tile(M, 512)
    const = lambda i: (0, 0)
    return pl.pallas_call(
        functools.partial(_mla_q_kernel, n_heads),
        grid=(M // tm,),
        in_specs=[
            pl.BlockSpec((tm, D), lambda i: (i, 0)),
            pl.BlockSpec((D, R), const),
            pl.BlockSpec((1, R), const),
            pl.BlockSpec((R, n_heads * QK_PAD), const),
            pl.BlockSpec((R, n_heads * LANES), const),
            pl.BlockSpec((tm, LANES), lambda i: (i, 0)),
            pl.BlockSpec((tm, LANES), lambda i: (i, 0)),
        ],
        out_specs=pl.BlockSpec((tm, n_heads * QK_PAD), lambda i: (i, 0)),
        out_shape=jax.ShapeDtypeStruct((M, n_heads * QK_PAD), BF16),
        compiler_params=_cparams(("parallel",), 56),
        name="mla_q",
    )(h, w_cq, q_norm_g.reshape(1, R), w_a, w_b, cos_t, sin_t)


def _mla_kv_kernel(n_heads, h_ref, wc_ref, g_ref, wk_ref, wv_ref, wra_ref, wrb_ref,
                   cos_ref, sin_ref, k_ref, v_ref):
    hb = h_ref[...]
    c_kv = jnp.dot(hb, wc_ref[...], preferred_element_type=F32)
    n = _rms(c_kv, g_ref[...]).astype(BF16)
    k_nope = jnp.dot(n, wk_ref[...], preferred_element_type=F32)
    v_ref[...] = jnp.dot(n, wv_ref[...], preferred_element_type=F32).astype(v_ref.dtype)
    ra = jnp.dot(hb, wra_ref[...], preferred_element_type=F32)
    rb = jnp.dot(hb, wrb_ref[...], preferred_element_type=F32)
    k_rope = (ra * cos_ref[...] + rb * sin_ref[...]).astype(k_ref.dtype)
    for hd in range(n_heads):
        o = hd * QK_PAD
        k_ref[:, o:o + LANES] = k_nope[:, hd * LANES:(hd + 1) * LANES].astype(k_ref.dtype)
        k_ref[:, o + LANES:o + QK_PAD] = k_rope


def _mla_kv_call(h, w_ckv, kv_norm_g, w_k, w_v, w_ra, w_rb, cos_t, sin_t, n_heads):
    M, D = h.shape
    R = w_ckv.shape[1]
    tm = _tile(M, 512)
    const = lambda i: (0, 0)
    return pl.pallas_call(
        functools.partial(_mla_kv_kernel, n_heads),
        grid=(M // tm,),
        in_specs=[
            pl.BlockSpec((tm, D), lambda i: (i, 0)),
            pl.BlockSpec((D, R), const),
            pl.BlockSpec((1, R), const),
            pl.BlockSpec((R, n_heads * LANES), const),
            pl.BlockSpec((R, n_heads * LANES), const),
            pl.BlockSpec((D, LANES), const),
            pl.BlockSpec((D, LANES), const),
            pl.BlockSpec((tm, LANES), lambda i: (i, 0)),
            pl.BlockSpec((tm, LANES), lambda i: (i, 0)),
        ],
        out_specs=[
            pl.BlockSpec((tm, n_heads * QK_PAD), lambda i: (i, 0)),
            pl.BlockSpec((tm, n_heads * LANES), lambda i: (i, 0)),
        ],
        out_shape=[
            jax.ShapeDtypeStruct((M, n_heads * QK_PAD), BF16),
            jax.ShapeDtypeStruct((M, n_heads * LANES), BF16),
        ],
        compiler_params=_cparams(("parallel",), 56),
        name="mla_kv",
    )(h, w_ckv, kv_norm_g.reshape(1, R), w_k, w_v, w_ra, w_rb, cos_t, sin_t)


def _fox_prep_kernel(n_heads, inv_scale, h_ref, wf_ref, bf_ref, q_ref, k_ref,
                     qo_ref, ko_ref, carry_ref):
    @pl.when(pl.program_id(1) == 0)
    def _():
        carry_ref[...] = jnp.zeros_like(carry_ref)

    f = jnp.dot(h_ref[...], wf_ref[...], preferred_element_type=F32) + bf_ref[...]
    log_f = jax.nn.log_sigmoid(f)
    ts = log_f.shape[0]
    row = lax.broadcasted_iota(jnp.int32, (ts, ts), 0)
    col = lax.broadcasted_iota(jnp.int32, (ts, ts), 1)
    tri = jnp.where(col <= row, 1.0, 0.0).astype(BF16)
    cum = carry_ref[0:1, :]
    for part in _split3(log_f):
        cum = cum + jnp.dot(tri, part, preferred_element_type=F32)
    carry_ref[...] = jnp.broadcast_to(cum[ts - 1:ts, :], carry_ref.shape)

    lane = lax.broadcasted_iota(jnp.int32, (ts, LANES), 1)
    cum_s = cum * inv_scale
    for hd in range(n_heads):
        colv = jnp.broadcast_to(cum_s[:, hd:hd + 1], (ts, LANES))
        hi, mid, lo = (part.astype(F32) for part in _split3(colv))
        q_aug = jnp.where(lane == 0, hi, jnp.where(lane == 1, mid, jnp.where(
            lane == 2, lo, jnp.where(lane < 6, 1.0, 0.0))))
        k_aug = jnp.where(lane < 3, 1.0, jnp.where(lane == 3, -hi, jnp.where(
            lane == 4, -mid, jnp.where(lane == 5, -lo, 0.0))))
        o = hd * QK_PAD
        qo_ref[:, o:o + LANES] = q_ref[:, hd * LANES:(hd + 1) * LANES]
        qo_ref[:, o + LANES:o + QK_PAD] = q_aug.astype(BF16)
        ko_ref[:, o:o + LANES] = k_ref[:, hd * LANES:(hd + 1) * LANES]
        ko_ref[:, o + LANES:o + QK_PAD] = k_aug.astype(BF16)


def _fox_prep_call(h, u, w_f, b_f, B, S, n_heads, q_blk, k_blk, scale):
    M, D = h.shape
    W = n_heads * LANES
    ts = _tile(S, 512)
    nb = S // ts
    row = lambda b, i: (b * nb + i, 0)
    return pl.pallas_call(
        functools.partial(_fox_prep_kernel, n_heads, 1.0 / scale),
        grid=(B, nb),
        in_specs=[
            pl.BlockSpec((ts, D), row),
            pl.BlockSpec((D, LANES), lambda b, i: (0, 0)),
            pl.BlockSpec((1, LANES), lambda b, i: (0, 0)),
            pl.BlockSpec((ts, W), lambda b, i: (b * nb + i, q_blk)),
            pl.BlockSpec((ts, W), lambda b, i: (b * nb + i, k_blk)),
        ],
        out_specs=[
            pl.BlockSpec((ts, n_heads * QK_PAD), row),
            pl.BlockSpec((ts, n_heads * QK_PAD), row),
        ],
        out_shape=[
            jax.ShapeDtypeStruct((M, n_heads * QK_PAD), BF16),
            jax.ShapeDtypeStruct((M, n_heads * QK_PAD), BF16),
        ],
        scratch_shapes=[pltpu.VMEM((8, LANES), F32)],
        compiler_params=_cparams(("parallel", "arbitrary")),
        name="fox_prep",
    )(h, w_f, b_f, u, u)


def _flash_kernel(scale_log2, t, hp, q_ref, k_ref, v_ref, o_ref, m_ref, l_ref, acc_ref):
    i = pl.program_id(2)
    m_ref[...] = jnp.full_like(m_ref, -jnp.inf)
    l_ref[...] = jnp.zeros_like(l_ref)
    acc_ref[...] = jnp.zeros_like(acc_ref)

    def tile(j, masked):
        off = pl.multiple_of(j * t, t)
        for hh in range(hp):
            q = q_ref[:, hh * QK_PAD:(hh + 1) * QK_PAD]
            k = k_ref[pl.ds(off, t), hh * QK_PAD:(hh + 1) * QK_PAD]
            v = v_ref[pl.ds(off, t), hh * HEAD_DIM:(hh + 1) * HEAD_DIM]
            s = lax.dot_general(q, k, (((1,), (1,)), ((), ())),
                                preferred_element_type=F32) * scale_log2
            if masked:
                row = lax.broadcasted_iota(jnp.int32, s.shape, 0)
                col = lax.broadcasted_iota(jnp.int32, s.shape, 1)
                s = jnp.where(col <= row, s, -jnp.inf)
            parts = [s[:, c * LANES:(c + 1) * LANES] for c in range(t // LANES)]
            part_max = parts[0]
            for part in parts[1:]:
                part_max = jnp.maximum(part_max, part)
            m_prev = m_ref[hh]
            m_new = jnp.maximum(m_prev, jnp.max(part_max, axis=-1, keepdims=True))
            alpha = jnp.exp2(m_prev - m_new)
            probs = [jnp.exp2(part - m_new) for part in parts]
            l_part = probs[0]
            for pr in probs[1:]:
                l_part = l_part + pr
            l_ref[hh] = alpha * l_ref[hh] + l_part
            p = jnp.concatenate(probs, axis=1).astype(v.dtype)
            acc_ref[hh] = alpha * acc_ref[hh] + jnp.dot(p, v, preferred_element_type=F32)
            m_ref[hh] = m_new

    def body(j, c):
        tile(j, False)
        return c

    lax.fori_loop(0, i, body, 0)
    tile(i, True)
    for hh in range(hp):
        l_row = jnp.sum(l_ref[hh], axis=-1, keepdims=True)
        o_ref[:, hh * HEAD_DIM:(hh + 1) * HEAD_DIM] = (acc_ref[hh] / l_row).astype(o_ref.dtype)


def _flash_call(q, k, v, B, S, n_heads, scale, v_blk0=0, name="flash"):
    M = q.shape[0]
    t = _tile(S, 512)
    nq = S // t
    hp = 2
    assert n_heads % hp == 0 and v_blk0 % hp == 0
    return pl.pallas_call(
        functools.partial(_flash_kernel, scale * 1.4426950408889634, t, hp),
        grid=(B, n_heads // hp, nq),
        in_specs=[
            pl.BlockSpec((t, hp * QK_PAD), lambda b, h, i: (b * nq + i, h)),
            pl.BlockSpec((S, hp * QK_PAD), lambda b, h, i: (b, h)),
            pl.BlockSpec((S, hp * HEAD_DIM), lambda b, h, i: (b, v_blk0 // hp + h)),
        ],
        out_specs=pl.BlockSpec((t, hp * HEAD_DIM), lambda b, h, i: (b * nq + i, h)),
        out_shape=jax.ShapeDtypeStruct((M, n_heads * HEAD_DIM), BF16),
        scratch_shapes=[
            pltpu.VMEM((hp, t, LANES), F32),
            pltpu.VMEM((hp, t, LANES), F32),
            pltpu.VMEM((hp, t, HEAD_DIM), F32),
        ],
        compiler_params=_cparams(("parallel", "parallel", "arbitrary")),
        name=name,
    )(q, k, v)


def _merge_kernel(n_br, h_ref, wg_ref, bg_ref, *rest):
    br_refs = rest[:n_br]
    wb_ref = rest[n_br]
    o_ref = rest[n_br + 1]
    hb = h_ref[...]
    merged = None
    for n in range(n_br):
        gate = jax.nn.sigmoid(jnp.dot(hb, wg_ref[n], preferred_element_type=F32) + bg_ref[n])
        proj = jnp.dot(br_refs[n][...], wb_ref[n], preferred_element_type=F32)
        merged = gate * proj if merged is None else merged + gate * proj
    o_ref[...] = merged.astype(o_ref.dtype)


def _merge_call(h, w_gate, b_gate, branches, w_branch):
    M, D = h.shape
    n_br, W, _ = w_branch.shape
    tm = _tile(M, 512)
    tn = _tile(D, 256)
    in_specs = [
        pl.BlockSpec((tm, D), lambda i, j: (i, 0)),
        pl.BlockSpec((n_br, D, tn), lambda i, j: (0, 0, j)),
        pl.BlockSpec((n_br, 1, tn), lambda i, j: (0, 0, j)),
    ]
    in_specs += [pl.BlockSpec((tm, W), lambda i, j: (i, 0)) for _ in range(n_br)]
    in_specs += [pl.BlockSpec((n_br, W, tn), lambda i, j: (0, 0, j))]
    return pl.pallas_call(
        functools.partial(_merge_kernel, n_br),
        grid=(M // tm, D // tn),
        in_specs=in_specs,
        out_specs=pl.BlockSpec((tm, tn), lambda i, j: (i, j)),
        out_shape=jax.ShapeDtypeStruct((M, D), BF16),
        compiler_params=_cparams(("parallel", "parallel"), 56),
        name="gated_merge",
    )(h, w_gate, b_gate.reshape(n_br, 1, D), *branches, w_branch)


def _post_kernel(x_ref, y_ref, g_ref, gate_ref, o_ref):
    o_ref[0] = x_ref[0] + gate_ref[0] * _rms(y_ref[0], g_ref[...])


def _post_call(x, y, g, gate):
    B, S, D = x.shape
    ts = _tile(S, 256)
    blk = pl.BlockSpec((1, ts, D), lambda b, i: (b, i, 0))
    return pl.pallas_call(
        _post_kernel,
        grid=(B, S // ts),
        in_specs=[blk, blk, pl.BlockSpec((1, D), lambda b, i: (0, 0)),
                  pl.BlockSpec((1, 1, D), lambda b, i: (b, 0, 0))],
        out_specs=blk,
        out_shape=jax.ShapeDtypeStruct((B, S, D), F32),
        compiler_params=_cparams(("parallel", "parallel")),
        name="post_residual",
    )(x, y.reshape(B, S, D), g.reshape(1, D), gate)


def _combine_post_kernel(x_ref, y_ref, p_ref, g_ref, gate_ref, o_ref):
    p = p_ref[...]
    ts = p.shape[0]
    y = _load_token_major(y_ref, ts, lead=0) * p[:, 0:1]
    for k in range(1, TOP_K):
        y = y + _load_token_major(y_ref, ts, lead=k) * p[:, k:k + 1]
    o_ref[0] = x_ref[0] + gate_ref[0] * _rms(y, g_ref[...])


def _combine_post_call(x, y_tok, top_p, g, gate):
    B, S, D = x.shape
    ts = _tile(S, 128)
    nb = S // ts
    blk = pl.BlockSpec((1, ts, D), lambda b, i: (b, i, 0))
    return pl.pallas_call(
        _combine_post_kernel,
        grid=(B, nb),
        in_specs=[blk,
                  pl.BlockSpec((TOP_K, ts * (D // LANES), LANES), lambda b, i: (0, b * nb + i, 0)),
                  pl.BlockSpec((ts, LANES), lambda b, i: (b * nb + i, 0)),
                  pl.BlockSpec((1, D), lambda b, i: (0, 0)),
                  pl.BlockSpec((1, 1, D), lambda b, i: (b, 0, 0))],
        out_specs=blk,
        out_shape=jax.ShapeDtypeStruct((B, S, D), F32),
        compiler_params=_cparams(("parallel", "parallel")),
        name="combine_post",
    )(x, y_tok, top_p, g.reshape(1, D), gate)


def _rank_kernel(e_ref, rank_ref, cnt_ref, carry_ref):
    @pl.when(pl.program_id(0) == 0)
    def _():
        carry_ref[...] = jnp.zeros_like(carry_ref)

    e = e_ref[...]
    tb = e.shape[0]
    lane = lax.broadcasted_iota(jnp.int32, e.shape, 1)
    sel = [jnp.broadcast_to(e[:, k:k + 1], e.shape) == lane for k in range(TOP_K)]
    onehot = jnp.zeros(e.shape, F32)
    for s in sel:
        onehot = onehot + jnp.where(s, 1.0, 0.0)
    row = lax.broadcasted_iota(jnp.int32, (tb, tb), 0)
    col = lax.broadcasted_iota(jnp.int32, (tb, tb), 1)
    strict = jnp.where(col < row, 1.0, 0.0).astype(BF16)
    before = carry_ref[0:1, :] + jnp.dot(strict, onehot.astype(BF16), preferred_element_type=F32)
    rank = jnp.zeros(e.shape, F32)
    for k in range(TOP_K):
        r_k = jnp.sum(jnp.where(sel[k], before, 0.0), axis=-1, keepdims=True)
        rank = jnp.where(lane == k, r_k, rank)
    rank_ref[...] = rank.astype(jnp.int32)
    total = carry_ref[0:1, :] + jnp.sum(onehot, axis=0, keepdims=True)
    carry_ref[...] = jnp.broadcast_to(total, carry_ref.shape)
    cnt_ref[...] = jnp.broadcast_to(total, cnt_ref.shape).astype(jnp.int32)


def _rank_call(top_e):
    N = top_e.shape[0]
    tb = _tile(N, 512)
    return pl.pallas_call(
        _rank_kernel,
        grid=(N // tb,),
        in_specs=[pl.BlockSpec((tb, LANES), lambda i: (i, 0))],
        out_specs=[pl.BlockSpec((tb, LANES), lambda i: (i, 0)),
                   pl.BlockSpec((8, LANES), lambda i: (0, 0))],
        out_shape=[jax.ShapeDtypeStruct((N, LANES), jnp.int32),
                   jax.ShapeDtypeStruct((8, LANES), jnp.int32)],
        scratch_shapes=[pltpu.VMEM((8, LANES), F32)],
        compiler_params=_cparams(("arbitrary",)),
        name="route_rank",
    )(top_e)


def _row_copy(ch, src, dst, s_row, d_row, sem):
    return pltpu.make_async_copy(src.at[pl.ds(pl.multiple_of(s_row * ch, ch), ch)],
                                 dst.at[pl.ds(pl.multiple_of(d_row * ch, ch), ch)], sem)


def _dispatch_kernel(n_tok, n_experts, n_blocks, ch, dest_ref, zlo_ref, zhi_ref, nblk_ref,
                     h_ref, xg_ref, zero_ref, sem):
    zero_ref[...] = jnp.zeros_like(zero_ref)

    def tok_body(t, c):
        for k in range(TOP_K):
            _row_copy(ch, h_ref, xg_ref, t, dest_ref[t * TOP_K + k], sem).start()
        return c

    lax.fori_loop(0, n_tok, tok_body, 0)

    def exp_body(e, c):
        def row_body(r, c2):
            _row_copy(ch, zero_ref, xg_ref, 0, r, sem).start()
            return c2
        return lax.fori_loop(zlo_ref[e], zhi_ref[e], row_body, c)

    lax.fori_loop(0, n_experts, exp_body, 0)

    blk = EXPERT_ROWS * ch

    def block_copy(b):
        return pltpu.make_async_copy(zero_ref, xg_ref.at[pl.ds(pl.multiple_of(b * blk, blk), blk)], sem)

    def tail_body(b, c):
        block_copy(b).start()
        return c

    lax.fori_loop(nblk_ref[0], n_blocks, tail_body, 0)

    def wait_body(b, c):
        block_copy(b).wait()
        return c

    lax.fori_loop(0, n_blocks, wait_body, 0)


def _dispatch_call(h_tm, dest_flat, zlo, zhi, n_used, n_tok, n_rows):
    ch = h_tm.shape[0] // n_tok
    E = zlo.shape[0]
    grid_spec = pltpu.PrefetchScalarGridSpec(
        num_scalar_prefetch=4,
        grid=(1,),
        in_specs=[pl.BlockSpec(memory_space=pl.ANY)],
        out_specs=pl.BlockSpec(memory_space=pl.ANY),
        scratch_shapes=[pltpu.VMEM((EXPERT_ROWS * ch, LANES), F32), pltpu.SemaphoreType.DMA(())],
    )
    return pl.pallas_call(
        functools.partial(_dispatch_kernel, n_tok, E, n_rows // EXPERT_ROWS, ch),
        grid_spec=grid_spec,
        out_shape=jax.ShapeDtypeStruct((n_rows * ch, LANES), F32),
        compiler_params=pltpu.CompilerParams(dimension_semantics=("arbitrary",),
                                             has_side_effects=True),
        name="moe_dispatch",
    )(dest_flat, zlo, zhi, n_used, h_tm)


def _gather_kernel(n_tok, ch, rows_per_wait, dest_ref, y_ref, o_ref, sem):
    def tok_body(t, c):
        for k in range(TOP_K):
            _row_copy(ch, y_ref, o_ref, dest_ref[t * TOP_K + k], k * n_tok + t, sem).start()
        return c

    lax.fori_loop(0, n_tok, tok_body, 0)

    def wait_body(b, c):
        pltpu.make_async_copy(o_ref.at[pl.ds(0, rows_per_wait * ch)],
                              o_ref.at[pl.ds(0, rows_per_wait * ch)], sem).wait()
        return c

    lax.fori_loop(0, (n_tok * TOP_K) // rows_per_wait, wait_body, 0)


def _gather_call(y_grouped, dest_flat, n_tok, n_rows):
    ch = y_grouped.shape[0] // n_rows
    rows_per_wait = _tile(n_tok * TOP_K, 256)
    grid_spec = pltpu.PrefetchScalarGridSpec(
        num_scalar_prefetch=1,
        grid=(1,),
        in_specs=[pl.BlockSpec(memory_space=pl.ANY)],
        out_specs=pl.BlockSpec(memory_space=pl.ANY),
        scratch_shapes=[pltpu.SemaphoreType.DMA(())],
    )
    return pl.pallas_call(
        functools.partial(_gather_kernel, n_tok, ch, rows_per_wait),
        grid_spec=grid_spec,
        out_shape=jax.ShapeDtypeStruct((TOP_K * n_tok * ch, LANES), F32),
        compiler_params=pltpu.CompilerParams(dimension_semantics=("arbitrary",),
                                             has_side_effects=True),
        name="moe_gather",
    )(dest_flat, y_grouped)


def _expert_kernel(f_dim, be_ref, nu_ref, x_ref, wgu_ref, bgu_ref, wd_ref, bd_ref, o_ref):
    i = pl.program_id(0)

    @pl.when(i < nu_ref[0])
    def _():
        xb = _load_token_major(x_ref, EXPERT_ROWS).astype(BF16)
        gu = jnp.dot(xb, wgu_ref[0], preferred_element_type=F32) + bgu_ref[0]
        g = jnp.minimum(gu[:, :f_dim], SWIGLU_LIMIT)
        up = jnp.clip(gu[:, f_dim:], -SWIGLU_LIMIT, SWIGLU_LIMIT)
        act = g * jax.nn.sigmoid(SWIGLU_ALPHA * g) * (up + 1.0)
        y = jnp.dot(act.astype(BF16), wd_ref[0], preferred_element_type=F32) + bd_ref[0]
        _store_token_major(o_ref, y)

    @pl.when(i >= nu_ref[0])
    def _():
        o_ref[...] = jnp.zeros_like(o_ref)


def _expert_call(x_grouped, block_e, n_used, w_gu, b_gu, w_d, b_d):
    E, D, F2 = w_gu.shape
    F = F2 // 2
    ch = D // LANES
    n_rows = x_grouped.shape[0] // ch
    n_blocks = n_rows // EXPERT_ROWS
    grid_spec = pltpu.PrefetchScalarGridSpec(
        num_scalar_prefetch=2,
        grid=(n_blocks,),
        in_specs=[
            pl.BlockSpec((EXPERT_ROWS * ch, LANES), lambda i, be, nu: (jnp.minimum(i, nu[0] - 1), 0)),
            pl.BlockSpec((1, D, F2), lambda i, be, nu: (be[i], 0, 0)),
            pl.BlockSpec((1, 1, F2), lambda i, be, nu: (be[i], 0, 0)),
            pl.BlockSpec((1, F, D), lambda i, be, nu: (be[i], 0, 0)),
            pl.BlockSpec((1, 1, D), lambda i, be, nu: (be[i], 0, 0)),
        ],
        out_specs=pl.BlockSpec((EXPERT_ROWS * ch, LANES), lambda i, be, nu: (i, 0)),
    )
    return pl.pallas_call(
        functools.partial(_expert_kernel, F),
        grid_spec=grid_spec,
        out_shape=jax.ShapeDtypeStruct((n_rows * ch, LANES), F32),
        compiler_params=_cparams(("arbitrary",), 56),
        name="experts",
    )(block_e, n_used, x_grouped, w_gu, b_gu.reshape(E, 1, F2), w_d, b_d.reshape(E, 1, D))


def _mixer_weights(D, w_mix_in, w_uq, w_ukv, fox_f_bias):
    Wb = D // 4
    H = Wb // HEAD_DIM
    q_lora = 3 * D // 16
    kv_lora = D // 8
    half = ROPE_DIM // 2
    o_cq = 3 * Wb
    o_ckv = o_cq + q_lora
    o_kr = o_ckv + kv_lora
    o_fox = o_kr + ROPE_DIM
    o_f = o_fox + 3 * Wb
    w_convfox = jnp.concatenate([w_mix_in[:, :o_cq], w_mix_in[:, o_fox:o_f]], axis=1).astype(BF16)
    w_cq = w_mix_in[:, o_cq:o_ckv].astype(BF16)
    w_ckv = w_mix_in[:, o_ckv:o_kr].astype(BF16)
    w_kr = w_mix_in[:, o_kr:o_fox]
    zpad = jnp.zeros((D, LANES - ROPE_DIM), F32)
    w_kra = jnp.concatenate([w_kr, zpad], axis=1).astype(BF16)
    w_krb = jnp.concatenate([-w_kr[:, half:], w_kr[:, :half], zpad], axis=1).astype(BF16)
    w_f = jnp.pad(w_mix_in[:, o_f:o_f + H], ((0, 0), (0, LANES - H))).astype(BF16)
    b_f = jnp.pad(fox_f_bias, (0, LANES - H)).reshape(1, LANES)

    uq = w_uq.reshape(q_lora, H, HEAD_DIM + ROPE_DIM)
    zq = jnp.zeros((q_lora, H, QK_PAD - HEAD_DIM - ROPE_DIM), F32)
    w_qa = jnp.concatenate([uq, zq], axis=2).reshape(q_lora, H * QK_PAD).astype(BF16)
    r1 = uq[:, :, HEAD_DIM:HEAD_DIM + half]
    r2 = uq[:, :, HEAD_DIM + half:]
    w_qb = jnp.concatenate([-r2, r1, zq], axis=2).reshape(q_lora, H * LANES).astype(BF16)

    ukv = w_ukv.reshape(kv_lora, H, 2 * HEAD_DIM)
    w_k = ukv[:, :, :HEAD_DIM].reshape(kv_lora, H * HEAD_DIM).astype(BF16)
    w_v = ukv[:, :, HEAD_DIM:].reshape(kv_lora, H * HEAD_DIM).astype(BF16)
    return dict(w_convfox=w_convfox, w_cq=w_cq, w_ckv=w_ckv, w_kra=w_kra, w_krb=w_krb,
                w_f=w_f, b_f=b_f, w_qa=w_qa, w_qb=w_qb, w_k=w_k, w_v=w_v)


def _mixer(h, B, S, D, rope_cos, rope_sin, mw, conv_w, q_norm_g, kv_norm_g,
           w_branch, w_merge_gate, b_merge_gate, w_o):
    Wb = D // 4
    H = Wb // HEAD_DIM
    u = _mm(h, mw["w_convfox"], BF16, tn_pref=512, name="mix_in")
    o_a = _conv_call(u, conv_w, B, S, Wb)

    q_mla = _mla_q_call(h, mw["w_cq"], q_norm_g, mw["w_qa"], mw["w_qb"], rope_cos, rope_sin, H)
    k_mla, v_mla = _mla_kv_call(h, mw["w_ckv"], kv_norm_g, mw["w_k"], mw["w_v"],
                                mw["w_kra"], mw["w_krb"], rope_cos, rope_sin, H)
    o_b = _flash_call(q_mla, k_mla, v_mla, B, S, H, (HEAD_DIM + ROPE_DIM) ** -0.5, name="flash_mla")

    fox_scale = HEAD_DIM ** -0.5
    q_fox, k_fox = _fox_prep_call(h, u, mw["w_f"], mw["b_f"], B, S, H, 3, 4, fox_scale)
    o_c = _flash_call(q_fox, k_fox, u, B, S, H, fox_scale, v_blk0=5 * H, name="flash_fox")

    merged = _merge_call(h, w_merge_gate.astype(BF16), b_merge_gate, (o_a, o_b, o_c),
                         w_branch.astype(BF16))
    return _mm(merged, w_o.astype(BF16), F32, name="mix_out")


def _moe(x, g_pre, scale, shift, g_post, gate, w_router, b_router, w_gate_up, b_gate_up,
         w_down, b_down):
    B, S, D = x.shape
    N = B * S
    E = w_router.shape[1]
    h, top_e, top_p = _norm_router_call(x, g_pre, scale, shift, w_router, b_router)
    rank, counts8 = _rank_call(top_e)

    counts = counts8[0, :E]
    padded = (counts + EXPERT_ROWS - 1) // EXPERT_ROWS * EXPERT_ROWS
    pad_end = jnp.cumsum(padded)
    pad_start = pad_end - padded
    e_flat = top_e[:, :TOP_K].reshape(-1)
    dest = pad_start[e_flat] + rank[:, :TOP_K].reshape(-1)
    n_blocks = (N * TOP_K + E * (EXPERT_ROWS - 1) + EXPERT_ROWS - 1) // EXPERT_ROWS
    n_rows = n_blocks * EXPERT_ROWS
    block_start = jnp.arange(n_blocks, dtype=jnp.int32) * EXPERT_ROWS
    block_e = jnp.minimum(
        jnp.sum((pad_end[None, :] <= block_start[:, None]).astype(jnp.int32), axis=1), E - 1)
    n_used = (pad_end[-1:] // EXPERT_ROWS).astype(jnp.int32)

    x_grouped = _dispatch_call(h, dest.astype(jnp.int32), (pad_start + counts).astype(jnp.int32),
                               pad_end.astype(jnp.int32), n_used, N, n_rows)
    y_grouped = _expert_call(x_grouped, block_e, n_used, w_gate_up.astype(BF16), b_gate_up,
                             w_down.astype(BF16), b_down)
    y_tok = _gather_call(y_grouped, dest.astype(jnp.int32), N, n_rows)
    y_tok = y_tok.reshape(TOP_K, N * (D // LANES), LANES)
    return _combine_post_call(x, y_tok, top_p, g_post, gate)


def kernel(x, c, positions, w_ada, b_ada, g_mix_pre, g_mix_post, g_ffn_pre, g_ffn_post, w_mix_in, conv_w, q_norm_g, kv_norm_g, w_uq, w_ukv, fox_f_bias, w_branch, w_merge_gate, b_merge_gate, w_o, w_router, b_router, w_gate_up, b_gate_up, w_down, b_down):
    B, S, D = x.shape
    L = w_ada.shape[0]
    N = B * S

    c8 = jnp.pad(c, ((0, 8 - B), (0, 0)))
    ada = _ada(c8, w_ada, b_ada)[:, :B].reshape(L, B, N_ADA, 1, D)

    half = ROPE_DIM // 2
    inv_freq = ROPE_THETA ** (-jnp.arange(half, dtype=F32) / half)
    ang = positions.astype(F32).reshape(N, 1) * inv_freq
    zer = jnp.zeros((N, LANES - ROPE_DIM), F32)
    rope_cos = jnp.concatenate([jnp.cos(ang), jnp.cos(ang), zer], axis=1)
    rope_sin = jnp.concatenate([jnp.sin(ang), jnp.sin(ang), zer], axis=1)

    for l in range(L):
        shift_m, scale_m, gate_m = ada[l, :, 0], ada[l, :, 1], ada[l, :, 2]
        shift_f, scale_f, gate_f = ada[l, :, 3], ada[l, :, 4], ada[l, :, 5]

        h = _norm_mod_call(x, g_mix_pre[l], scale_m, shift_m).reshape(N, D)
        mw = _mixer_weights(D, w_mix_in[l], w_uq[l], w_ukv[l], fox_f_bias[l])
        y = _mixer(h, B, S, D, rope_cos, rope_sin, mw, conv_w[l], q_norm_g[l], kv_norm_g[l],
                   w_branch[l], w_merge_gate[l], b_merge_gate[l], w_o[l])
        x = _post_call(x, y, g_mix_post[l], gate_m)

        x = _moe(x, g_ffn_pre[l], scale_f, shift_f, g_ffn_post[l], gate_f, w_router[l],
                 b_router[l], w_gate_up[l], b_gate_up[l], w_down[l], b_down[l])
    return x
```

```python
import functools

import jax
import jax.numpy as jnp
from jax import lax
from jax.experimental import pallas as pl
from jax.experimental.pallas import tpu as pltpu

NORM_EPS = 1e-6
ROPE_THETA = 10000.0
HEAD_DIM = 128
ROPE_DIM = 64
CONV_K = 3
TOP_K = 4
SWIGLU_ALPHA = 1.702
SWIGLU_LIMIT = 7.0
N_ADA = 6
LANES = 128
QK_PAD = 2 * LANES
EXPERT_ROWS = 256
DMA_CHUNK_TOKENS = 64
MIB = 1024 * 1024

F32 = jnp.float32
BF16 = jnp.bfloat16


def _cparams(sem, vmem_mib=48):
    return pltpu.CompilerParams(dimension_semantics=sem, vmem_limit_bytes=vmem_mib * MIB)


def _tile(n, pref):
    t = min(pref, n)
    while n % t:
        t //= 2
    return t


def _split3(x):
    hi = x.astype(BF16)
    r1 = x - hi.astype(F32)
    mid = r1.astype(BF16)
    lo = (r1 - mid.astype(F32)).astype(BF16)
    return hi, mid, lo


def _ada_kernel(c_ref, w_ref, b_ref, o_ref):
    c = c_ref[...]
    c_act = (c * jax.nn.sigmoid(c)).astype(BF16)
    acc = jnp.dot(c_act, w_ref[0].astype(BF16), preferred_element_type=F32)
    o_ref[0] = acc + b_ref[0]


def _ada(c8, w_ada, b_ada):
    L, D, N = w_ada.shape
    tn = _tile(N, 1024)
    return pl.pallas_call(
        _ada_kernel,
        grid=(L, N // tn),
        in_specs=[
            pl.BlockSpec((8, D), lambda l, j: (0, 0)),
            pl.BlockSpec((1, D, tn), lambda l, j: (l, 0, j)),
            pl.BlockSpec((1, 1, tn), lambda l, j: (l, 0, j)),
        ],
        out_specs=pl.BlockSpec((1, 8, tn), lambda l, j: (l, 0, j)),
        out_shape=jax.ShapeDtypeStruct((L, 8, N), F32),
        compiler_params=_cparams(("parallel", "parallel"), 56),
        name="ada",
    )(c8, w_ada, b_ada.reshape(L, 1, N))


def _norm_mod(x, g, scale, shift):
    xf = x
    y = xf * lax.rsqrt(jnp.mean(xf * xf, axis=-1, keepdims=True) + NORM_EPS)
    return (y * g) * (1.0 + scale) + shift


def _norm_mod_kernel(x_ref, g_ref, sc_ref, sh_ref, o_ref):
    o_ref[0] = _norm_mod(x_ref[0], g_ref[...], sc_ref[0], sh_ref[0]).astype(o_ref.dtype)


def _norm_mod_call(x, g, scale, shift):
    B, S, D = x.shape
    ts = _tile(S, 256)
    return pl.pallas_call(
        _norm_mod_kernel,
        grid=(B, S // ts),
        in_specs=[
            pl.BlockSpec((1, ts, D), lambda b, i: (b, i, 0)),
            pl.BlockSpec((1, D), lambda b, i: (0, 0)),
            pl.BlockSpec((1, 1, D), lambda b, i: (b, 0, 0)),
            pl.BlockSpec((1, 1, D), lambda b, i: (b, 0, 0)),
        ],
        out_specs=pl.BlockSpec((1, ts, D), lambda b, i: (b, i, 0)),
        out_shape=jax.ShapeDtypeStruct((B, S, D), BF16),
        compiler_params=_cparams(("parallel", "parallel")),
        name="norm_mod",
    )(x, g.reshape(1, D), scale, shift)


def _norm_router_kernel(n_experts, x_ref, g_ref, sc_ref, sh_ref, wr_hi_ref, wr_lo_ref, br_ref,
                        h_ref, e_ref, p_ref):
    h = _norm_mod(x_ref[0], g_ref[...], sc_ref[0], sh_ref[0])
    h_ref[0] = h
    h_hi = h.astype(BF16)
    h_lo = (h - h_hi.astype(F32)).astype(BF16)
    logits = (jnp.dot(h_hi, wr_hi_ref[...], preferred_element_type=F32)
              + jnp.dot(h_hi, wr_lo_ref[...], preferred_element_type=F32)
              + jnp.dot(h_lo, wr_hi_ref[...], preferred_element_type=F32)) + br_ref[...]
    lane = lax.broadcasted_iota(jnp.int32, logits.shape, 1)
    vals = jnp.where(lane < n_experts, logits, -jnp.inf)
    tops, idxs = [], []
    for _ in range(TOP_K):
        m = jnp.max(vals, axis=-1, keepdims=True)
        idx = jnp.min(jnp.where(vals == m, lane, LANES), axis=-1, keepdims=True)
        tops.append(m)
        idxs.append(idx)
        vals = jnp.where(lane == idx, -jnp.inf, vals)
    exps = [jnp.exp(t - tops[0]) for t in tops]
    denom = exps[0]
    for e in exps[1:]:
        denom = denom + e
    e_out = jnp.zeros(logits.shape, jnp.int32)
    p_out = jnp.zeros(logits.shape, F32)
    for k in range(TOP_K):
        e_out = jnp.where(lane == k, idxs[k], e_out)
        p_out = jnp.where(lane == k, exps[k] / denom, p_out)
    e_ref[...] = e_out
    p_ref[...] = p_out


def _norm_router_call(x, g, scale, shift, w_router, b_router):
    B, S, D = x.shape
    E = w_router.shape[1]
    ts = _tile(S, 256)
    nb = S // ts
    wr = jnp.pad(w_router, ((0, 0), (0, LANES - E)))
    wr_hi = wr.astype(BF16)
    wr_lo = (wr - wr_hi.astype(F32)).astype(BF16)
    br = jnp.pad(b_router, (0, LANES - E)).reshape(1, LANES)
    return pl.pallas_call(
        functools.partial(_norm_router_kernel, E),
        grid=(B, nb),
        in_specs=[
            pl.BlockSpec((1, ts, D), lambda b, i: (b, i, 0)),
            pl.BlockSpec((1, D), lambda b, i: (0, 0)),
            pl.BlockSpec((1, 1, D), lambda b, i: (b, 0, 0)),
            pl.BlockSpec((1, 1, D), lambda b, i: (b, 0, 0)),
            pl.BlockSpec((D, LANES), lambda b, i: (0, 0)),
            pl.BlockSpec((D, LANES), lambda b, i: (0, 0)),
            pl.BlockSpec((1, LANES), lambda b, i: (0, 0)),
        ],
        out_specs=[
            pl.BlockSpec((1, ts, D), lambda b, i: (b, i, 0)),
            pl.BlockSpec((ts, LANES), lambda b, i: (b * nb + i, 0)),
            pl.BlockSpec((ts, LANES), lambda b, i: (b * nb + i, 0)),
        ],
        out_shape=[
            jax.ShapeDtypeStruct((B, S, D), F32),
            jax.ShapeDtypeStruct((B * S, LANES), jnp.int32),
            jax.ShapeDtypeStruct((B * S, LANES), F32),
        ],
        compiler_params=_cparams(("parallel", "parallel")),
        name="norm_router",
    )(x, g.reshape(1, D), scale, shift, wr_hi, wr_lo, br)


def _mm_kernel(a_ref, w_ref, o_ref):
    o_ref[...] = jnp.dot(a_ref[...], w_ref[...], preferred_element_type=F32).astype(o_ref.dtype)


def _mm(a, w, out_dtype, tm_pref=1024, tn_pref=512, name="mm"):
    M, K = a.shape
    N = w.shape[1]
    tm = _tile(M, tm_pref)
    tn = _tile(N, tn_pref)
    return pl.pallas_call(
        _mm_kernel,
        grid=(M // tm, N // tn),
        in_specs=[
            pl.BlockSpec((tm, K), lambda i, j: (i, 0)),
            pl.BlockSpec((K, tn), lambda i, j: (0, j)),
        ],
        out_specs=pl.BlockSpec((tm, tn), lambda i, j: (i, j)),
        out_shape=jax.ShapeDtypeStruct((M, N), out_dtype),
        compiler_params=_cparams(("parallel", "parallel"), 56),
        name=name,
    )(a, w)


def _conv_kernel(b_ref, c_ref, h_ref, w_ref, o_ref):
    g = c_ref[...].astype(F32) * h_ref[...].astype(F32)
    row = lax.broadcasted_iota(jnp.int32, g.shape, 0)
    g1 = jnp.where(row >= 1, pltpu.roll(g, 1, 0), 0.0)
    g2 = jnp.where(row >= 2, pltpu.roll(g, 2, 0), 0.0)
    w = w_ref[...]
    conv = w[0:1] * g2 + w[1:2] * g1 + w[2:3] * g
    o_ref[...] = (b_ref[...].astype(F32) * conv).astype(o_ref.dtype)


def _conv_call(u, conv_w, B, S, C):
    tc = _tile(C, 256)
    nc = C // tc
    w8 = jnp.pad(conv_w, ((0, 8 - CONV_K), (0, 0)))
    return pl.pallas_call(
        _conv_kernel,
        grid=(B, nc),
        in_specs=[
            pl.BlockSpec((S, tc), lambda b, j: (b, j)),
            pl.BlockSpec((S, tc), lambda b, j: (b, nc + j)),
            pl.BlockSpec((S, tc), lambda b, j: (b, 2 * nc + j)),
            pl.BlockSpec((8, tc), lambda b, j: (0, j)),
        ],
        out_specs=pl.BlockSpec((S, tc), lambda b, j: (b, j)),
        out_shape=jax.ShapeDtypeStruct((B * S, C), BF16),
        compiler_params=_cparams(("parallel", "parallel")),
        name="gated_conv",
    )(u, u, u, w8)


def _rms(x, g):
    return x * lax.rsqrt(jnp.mean(x * x, axis=-1, keepdims=True) + NORM_EPS) * g


def _mla_q_kernel(n_heads, h_ref, wc_ref, g_ref, wa_ref, wb_ref, cos_ref, sin_ref, q_ref):
    c_q = jnp.dot(h_ref[...], wc_ref[...], preferred_element_type=F32)
    n = _rms(c_q, g_ref[...]).astype(BF16)
    a = jnp.dot(n, wa_ref[...], preferred_element_type=F32)
    bm = jnp.dot(n, wb_ref[...], preferred_element_type=F32)
    cos = cos_ref[...]
    sin = sin_ref[...]
    for hd in range(n_heads):
        o = hd * QK_PAD
        q_ref[:, o:o + LANES] = a[:, o:o + LANES].astype(q_ref.dtype)
        rot = a[:, o + LANES:o + QK_PAD] * cos + bm[:, hd * LANES:(hd + 1) * LANES] * sin
        q_ref[:, o + LANES:o + QK_PAD] = rot.astype(q_ref.dtype)


def _mla_q_call(h, w_cq, q_norm_g, w_a, w_b, cos_t, sin_t, n_heads):
    M, D = h.shape
    R = w_cq.shape[1]
    tm = _tile(M, 512)
    const = lambda i: (0, 0)
    return pl.pallas_call(
        functools.partial(_mla_q_kernel, n_heads),
        grid=(M // tm,),
        in_specs=[
            pl.BlockSpec((tm, D), lambda i: (i, 0)),
            pl.BlockSpec((D, R), const),
            pl.BlockSpec((1, R), const),
            pl.BlockSpec((R, n_heads * QK_PAD), const),
            pl.BlockSpec((R, n_heads * LANES), const),
            pl.BlockSpec((tm, LANES), lambda i: (i, 0)),
            pl.BlockSpec((tm, LANES), lambda i: (i, 0)),
        ],
        out_specs=pl.BlockSpec((tm, n_heads * QK_PAD), lambda i: (i, 0)),
        out_shape=jax.ShapeDtypeStruct((M, n_heads * QK_PAD), BF16),
        compiler_params=_cparams(("parallel",), 56),
        name="mla_q",
    )(h, w_cq, q_norm_g.reshape(1, R), w_a, w_b, cos_t, sin_t)


def _mla_kv_kernel(n_heads, h_ref, wc_ref, g_ref, wk_ref, wv_ref, wra_ref, wrb_ref,
                   cos_ref, sin_ref, k_ref, v_ref):
    hb = h_ref[...]
    c_kv = jnp.dot(hb, wc_ref[...], preferred_element_type=F32)
    n = _rms(c_kv, g_ref[...]).astype(BF16)
    k_nope = jnp.dot(n, wk_ref[...], preferred_element_type=F32)
    v_ref[...] = jnp.dot(n, wv_ref[...], preferred_element_type=F32).astype(v_ref.dtype)
    ra = jnp.dot(hb, wra_ref[...], preferred_element_type=F32)
    rb = jnp.dot(hb, wrb_ref[...], preferred_element_type=F32)
    k_rope = (ra * cos_ref[...] + rb * sin_ref[...]).astype(k_ref.dtype)
    for hd in range(n_heads):
        o = hd * QK_PAD
        k_ref[:, o:o + LANES] = k_nope[:, hd * LANES:(hd + 1) * LANES].astype(k_ref.dtype)
        k_ref[:, o + LANES:o + QK_PAD] = k_rope


def _mla_kv_call(h, w_ckv, kv_norm_g, w_k, w_v, w_ra, w_rb, cos_t, sin_t, n_heads):
    M, D = h.shape
    R = w_ckv.shape[1]
    tm = _tile(M, 512)
    const = lambda i: (0, 0)
    return pl.pallas_call(
        functools.partial(_mla_kv_kernel, n_heads),
        grid=(M // tm,),
        in_specs=[
            pl.BlockSpec((tm, D), lambda i: (i, 0)),
            pl.BlockSpec((D, R), const),
            pl.BlockSpec((1, R), const),
            pl.BlockSpec((R, n_heads * LANES), const),
            pl.BlockSpec((R, n_heads * LANES), const),
            pl.BlockSpec((D, LANES), const),
            pl.BlockSpec((D, LANES), const),
            pl.BlockSpec((tm, LANES), lambda i: (i, 0)),
            pl.BlockSpec((tm, LANES), lambda i: (i, 0)),
        ],
        out_specs=[
            pl.BlockSpec((tm, n_heads * QK_PAD), lambda i: (i, 0)),
            pl.BlockSpec((tm, n_heads * LANES), lambda i: (i, 0)),
        ],
        out_shape=[
            jax.ShapeDtypeStruct((M, n_heads * QK_PAD), BF16),
            jax.ShapeDtypeStruct((M, n_heads * LANES), BF16),
        ],
        compiler_params=_cparams(("parallel",), 56),
        name="mla_kv",
    )(h, w_ckv, kv_norm_g.reshape(1, R), w_k, w_v, w_ra, w_rb, cos_t, sin_t)


def _fox_prep_kernel(n_heads, inv_scale, h_ref, wf_ref, bf_ref, q_ref, k_ref,
                     qo_ref, ko_ref, carry_ref):
    @pl.when(pl.program_id(1) == 0)
    def _():
        carry_ref[...] = jnp.zeros_like(carry_ref)

    f = jnp.dot(h_ref[...], wf_ref[...], preferred_element_type=F32) + bf_ref[...]
    log_f = jax.nn.log_sigmoid(f)
    ts = log_f.shape[0]
    row = lax.broadcasted_iota(jnp.int32, (ts, ts), 0)
    col = lax.broadcasted_iota(jnp.int32, (ts, ts), 1)
    tri = jnp.where(col <= row, 1.0, 0.0).astype(BF16)
    cum = carry_ref[0:1, :]
    for part in _split3(log_f):
        cum = cum + jnp.dot(tri, part, preferred_element_type=F32)
    carry_ref[...] = jnp.broadcast_to(cum[ts - 1:ts, :], carry_ref.shape)

    lane = lax.broadcasted_iota(jnp.int32, (ts, LANES), 1)
    cum_s = cum * inv_scale
    for hd in range(n_heads):
        colv = jnp.broadcast_to(cum_s[:, hd:hd + 1], (ts, LANES))
        hi, mid, lo = (part.astype(F32) for part in _split3(colv))
        q_aug = jnp.where(lane == 0, hi, jnp.where(lane == 1, mid, jnp.where(
            lane == 2, lo, jnp.where(lane < 6, 1.0, 0.0))))
        k_aug = jnp.where(lane < 3, 1.0, jnp.where(lane == 3, -hi, jnp.where(
            lane == 4, -mid, jnp.where(lane == 5, -lo, 0.0))))
        o = hd * QK_PAD
        qo_ref[:, o:o + LANES] = q_ref[:, hd * LANES:(hd + 1) * LANES]
        qo_ref[:, o + LANES:o + QK_PAD] = q_aug.astype(BF16)
        ko_ref[:, o:o + LANES] = k_ref[:, hd * LANES:(hd + 1) * LANES]
        ko_ref[:, o + LANES:o + QK_PAD] = k_aug.astype(BF16)


def _fox_prep_call(h, u, w_f, b_f, B, S, n_heads, q_blk, k_blk, scale):
    M, D = h.shape
    W = n_heads * LANES
    ts = _tile(S, 512)
    nb = S // ts
    row = lambda b, i: (b * nb + i, 0)
    return pl.pallas_call(
        functools.partial(_fox_prep_kernel, n_heads, 1.0 / scale),
        grid=(B, nb),
        in_specs=[
            pl.BlockSpec((ts, D), row),
            pl.BlockSpec((D, LANES), lambda b, i: (0, 0)),
            pl.BlockSpec((1, LANES), lambda b, i: (0, 0)),
            pl.BlockSpec((ts, W), lambda b, i: (b * nb + i, q_blk)),
            pl.BlockSpec((ts, W), lambda b, i: (b * nb + i, k_blk)),
        ],
        out_specs=[
            pl.BlockSpec((ts, n_heads * QK_PAD), row),
            pl.BlockSpec((ts, n_heads * QK_PAD), row),
        ],
        out_shape=[
            jax.ShapeDtypeStruct((M, n_heads * QK_PAD), BF16),
            jax.ShapeDtypeStruct((M, n_heads * QK_PAD), BF16),
        ],
        scratch_shapes=[pltpu.VMEM((8, LANES), F32)],
        compiler_params=_cparams(("parallel", "arbitrary")),
        name="fox_prep",
    )(h, w_f, b_f, u, u)


def _flash_kernel(scale_log2, t, hp, q_ref, k_ref, v_ref, o_ref, m_ref, l_ref, acc_ref):
    i = pl.program_id(2)
    m_ref[...] = jnp.full_like(m_ref, -jnp.inf)
    l_ref[...] = jnp.zeros_like(l_ref)
    acc_ref[...] = jnp.zeros_like(acc_ref)

    def tile(j, masked):
        off = pl.multiple_of(j * t, t)
        for hh in range(hp):
            q = q_ref[:, hh * QK_PAD:(hh + 1) * QK_PAD]
            k = k_ref[pl.ds(off, t), hh * QK_PAD:(hh + 1) * QK_PAD]
            v = v_ref[pl.ds(off, t), hh * HEAD_DIM:(hh + 1) * HEAD_DIM]
            s = lax.dot_general(q, k, (((1,), (1,)), ((), ())),
                                preferred_element_type=F32) * scale_log2
            if masked:
                row = lax.broadcasted_iota(jnp.int32, s.shape, 0)
                col = lax.broadcasted_iota(jnp.int32, s.shape, 1)
                s = jnp.where(col <= row, s, -jnp.inf)
            parts = [s[:, c * LANES:(c + 1) * LANES] for c in range(t // LANES)]
            part_max = parts[0]
            for part in parts[1:]:
                part_max = jnp.maximum(part_max, part)
            m_prev = m_ref[hh]
            m_new = jnp.maximum(m_prev, jnp.max(part_max, axis=-1, keepdims=True))
            alpha = jnp.exp2(m_prev - m_new)
            probs = [jnp.exp2(part - m_new) for part in parts]
            l_part = probs[0]
            for pr in probs[1:]:
                l_part = l_part + pr
            l_ref[hh] = alpha * l_ref[hh] + l_part
            p = jnp.concatenate(probs, axis=1).astype(v.dtype)
            acc_ref[hh] = alpha * acc_ref[hh] + jnp.dot(p, v, preferred_element_type=F32)
            m_ref[hh] = m_new

    def body(j, c):
        tile(j, False)
        return c

    lax.fori_loop(0, i, body, 0)
    tile(i, True)
    for hh in range(hp):
        l_row = jnp.sum(l_ref[hh], axis=-1, keepdims=True)
        o_ref[:, hh * HEAD_DIM:(hh + 1) * HEAD_DIM] = (acc_ref[hh] / l_row).astype(o_ref.dtype)


def _flash_call(q, k, v, B, S, n_heads, scale, v_blk0=0, name="flash"):
    M = q.shape[0]
    t = _tile(S, 512)
    nq = S // t
    hp = 4
    assert n_heads % hp == 0 and v_blk0 % hp == 0
    return pl.pallas_call(
        functools.partial(_flash_kernel, scale * 1.4426950408889634, t, hp),
        grid=(B, n_heads // hp, nq),
        in_specs=[
            pl.BlockSpec((t, hp * QK_PAD), lambda b, h, i: (b * nq + i, h)),
            pl.BlockSpec((S, hp * QK_PAD), lambda b, h, i: (b, h)),
            pl.BlockSpec((S, hp * HEAD_DIM), lambda b, h, i: (b, v_blk0 // hp + h)),
        ],
        out_specs=pl.BlockSpec((t, hp * HEAD_DIM), lambda b, h, i: (b * nq + i, h)),
        out_shape=jax.ShapeDtypeStruct((M, n_heads * HEAD_DIM), BF16),
        scratch_shapes=[
            pltpu.VMEM((hp, t, LANES), F32),
            pltpu.VMEM((hp, t, LANES), F32),
            pltpu.VMEM((hp, t, HEAD_DIM), F32),
        ],
        compiler_params=_cparams(("parallel", "parallel", "arbitrary")),
        name=name,
    )(q, k, v)


def _merge_kernel(n_br, h_ref, wg_ref, bg_ref, *rest):
    br_refs = rest[:n_br]
    wb_ref = rest[n_br]
    o_ref = rest[n_br + 1]
    hb = h_ref[...]
    merged = None
    for n in range(n_br):
        gate = jax.nn.sigmoid(jnp.dot(hb, wg_ref[n], preferred_element_type=F32) + bg_ref[n])
        proj = jnp.dot(br_refs[n][...], wb_ref[n], preferred_element_type=F32)
        merged = gate * proj if merged is None else merged + gate * proj
    o_ref[...] = merged.astype(o_ref.dtype)


def _merge_call(h, w_gate, b_gate, branches, w_branch):
    M, D = h.shape
    n_br, W, _ = w_branch.shape
    tm = _tile(M, 512)
    tn = _tile(D, 256)
    in_specs = [
        pl.BlockSpec((tm, D), lambda i, j: (i, 0)),
        pl.BlockSpec((n_br, D, tn), lambda i, j: (0, 0, j)),
        pl.BlockSpec((n_br, 1, tn), lambda i, j: (0, 0, j)),
    ]
    in_specs += [pl.BlockSpec((tm, W), lambda i, j: (i, 0)) for _ in range(n_br)]
    in_specs += [pl.BlockSpec((n_br, W, tn), lambda i, j: (0, 0, j))]
    return pl.pallas_call(
        functools.partial(_merge_kernel, n_br),
        grid=(M // tm, D // tn),
        in_specs=in_specs,
        out_specs=pl.BlockSpec((tm, tn), lambda i, j: (i, j)),
        out_shape=jax.ShapeDtypeStruct((M, D), BF16),
        compiler_params=_cparams(("parallel", "parallel"), 56),
        name="gated_merge",
    )(h, w_gate, b_gate.reshape(n_br, 1, D), *branches, w_branch)


def _post_kernel(x_ref, y_ref, g_ref, gate_ref, o_ref):
    o_ref[0] = x_ref[0] + gate_ref[0] * _rms(y_ref[0], g_ref[...])


def _post_call(x, y, g, gate):
    B, S, D = x.shape
    ts = _tile(S, 256)
    blk = pl.BlockSpec((1, ts, D), lambda b, i: (b, i, 0))
    return pl.pallas_call(
        _post_kernel,
        grid=(B, S // ts),
        in_specs=[blk, blk, pl.BlockSpec((1, D), lambda b, i: (0, 0)),
                  pl.BlockSpec((1, 1, D), lambda b, i: (b, 0, 0))],
        out_specs=blk,
        out_shape=jax.ShapeDtypeStruct((B, S, D), F32),
        compiler_params=_cparams(("parallel", "parallel")),
        name="post_residual",
    )(x, y.reshape(B, S, D), g.reshape(1, D), gate)


def _combine_post_kernel(x_ref, y_ref, p_ref, g_ref, gate_ref, o_ref):
    p = p_ref[...]
    y = y_ref[0] * p[:, 0:1]
    for k in range(1, TOP_K):
        y = y + y_ref[k] * p[:, k:k + 1]
    o_ref[0] = x_ref[0] + gate_ref[0] * _rms(y, g_ref[...])


def _combine_post_call(x, y_tok, top_p, g, gate):
    B, S, D = x.shape
    ts = _tile(S, 128)
    nb = S // ts
    blk = pl.BlockSpec((1, ts, D), lambda b, i: (b, i, 0))
    return pl.pallas_call(
        _combine_post_kernel,
        grid=(B, nb),
        in_specs=[blk,
                  pl.BlockSpec((TOP_K, ts, D), lambda b, i: (0, b * nb + i, 0)),
                  pl.BlockSpec((ts, LANES), lambda b, i: (b * nb + i, 0)),
                  pl.BlockSpec((1, D), lambda b, i: (0, 0)),
                  pl.BlockSpec((1, 1, D), lambda b, i: (b, 0, 0))],
        out_specs=blk,
        out_shape=jax.ShapeDtypeStruct((B, S, D), F32),
        compiler_params=_cparams(("parallel", "parallel")),
        name="combine_post",
    )(x, y_tok, top_p, g.reshape(1, D), gate)


def _rank_kernel(e_ref, rank_ref, cnt_ref, carry_ref):
    @pl.when(pl.program_id(0) == 0)
    def _():
        carry_ref[...] = jnp.zeros_like(carry_ref)

    e = e_ref[...]
    tb = e.shape[0]
    lane = lax.broadcasted_iota(jnp.int32, e.shape, 1)
    sel = [jnp.broadcast_to(e[:, k:k + 1], e.shape) == lane for k in range(TOP_K)]
    onehot = jnp.zeros(e.shape, F32)
    for s in sel:
        onehot = onehot + jnp.where(s, 1.0, 0.0)
    row = lax.broadcasted_iota(jnp.int32, (tb, tb), 0)
    col = lax.broadcasted_iota(jnp.int32, (tb, tb), 1)
    strict = jnp.where(col < row, 1.0, 0.0).astype(BF16)
    before = carry_ref[0:1, :] + jnp.dot(strict, onehot.astype(BF16), preferred_element_type=F32)
    rank = jnp.zeros(e.shape, F32)
    for k in range(TOP_K):
        r_k = jnp.sum(jnp.where(sel[k], before, 0.0), axis=-1, keepdims=True)
        rank = jnp.where(lane == k, r_k, rank)
    rank_ref[...] = rank.astype(jnp.int32)
    total = carry_ref[0:1, :] + jnp.sum(onehot, axis=0, keepdims=True)
    carry_ref[...] = jnp.broadcast_to(total, carry_ref.shape)
    cnt_ref[...] = jnp.broadcast_to(total, cnt_ref.shape).astype(jnp.int32)


def _rank_call(top_e):
    N = top_e.shape[0]
    tb = _tile(N, 512)
    return pl.pallas_call(
        _rank_kernel,
        grid=(N // tb,),
        in_specs=[pl.BlockSpec((tb, LANES), lambda i: (i, 0))],
        out_specs=[pl.BlockSpec((tb, LANES), lambda i: (i, 0)),
                   pl.BlockSpec((8, LANES), lambda i: (0, 0))],
        out_shape=[jax.ShapeDtypeStruct((N, LANES), jnp.int32),
                   jax.ShapeDtypeStruct((8, LANES), jnp.int32)],
        scratch_shapes=[pltpu.VMEM((8, LANES), F32)],
        compiler_params=_cparams(("arbitrary",)),
        name="route_rank",
    )(top_e)


def _row_copy(src, dst, s_row, d_row, sem):
    return pltpu.make_async_copy(src.at[pl.ds(s_row, 1)], dst.at[pl.ds(d_row, 1)], sem)


def _rows_wait(ref, n, sem):
    pltpu.make_async_copy(ref.at[pl.ds(0, n)], ref.at[pl.ds(0, n)], sem).wait()


def _chunked_token_copies(n_tok, chunk, issue_token, wait_chunk):
    def chunk_body(c, carry):
        def tok_body(i, c2):
            issue_token(c * chunk + i)
            return c2
        lax.fori_loop(0, chunk, tok_body, 0)

        @pl.when(c > 0)
        def _():
            wait_chunk()
        return carry

    lax.fori_loop(0, n_tok // chunk, chunk_body, 0)
    wait_chunk()


def _dispatch_kernel(n_tok, n_experts, n_blocks, chunk, dest_ref, zlo_ref, zhi_ref, nblk_ref,
                     h_ref, xg_ref, zero_ref, sem):
    zero_ref[...] = jnp.zeros_like(zero_ref)

    def issue_token(t):
        for k in range(TOP_K):
            _row_copy(h_ref, xg_ref, t, dest_ref[t * TOP_K + k], sem).start()

    _chunked_token_copies(n_tok, chunk, issue_token,
                          lambda: _rows_wait(xg_ref, chunk * TOP_K, sem))

    def exp_body(e, carry):
        def issue(r, c2):
            _row_copy(zero_ref, xg_ref, 0, r, sem).start()
            return c2

        def wait(r, c2):
            _row_copy(zero_ref, xg_ref, 0, r, sem).wait()
            return c2

        lax.fori_loop(zlo_ref[e], zhi_ref[e], issue, 0)
        lax.fori_loop(zlo_ref[e], zhi_ref[e], wait, 0)
        return carry

    lax.fori_loop(0, n_experts, exp_body, 0)

    def tail_body(b, carry):
        cp = pltpu.make_async_copy(
            zero_ref, xg_ref.at[pl.ds(pl.multiple_of(b * EXPERT_ROWS, EXPERT_ROWS), EXPERT_ROWS)], sem)
        cp.start()
        cp.wait()
        return carry

    lax.fori_loop(nblk_ref[0], n_blocks, tail_body, 0)


def _dispatch_call(h, dest_flat, zlo, zhi, n_used, n_rows):
    N, D = h.shape
    E = zlo.shape[0]
    grid_spec = pltpu.PrefetchScalarGridSpec(
        num_scalar_prefetch=4,
        grid=(1,),
        in_specs=[pl.BlockSpec(memory_space=pl.ANY)],
        out_specs=pl.BlockSpec(memory_space=pl.ANY),
        scratch_shapes=[pltpu.VMEM((EXPERT_ROWS, D), F32), pltpu.SemaphoreType.DMA(())],
    )
    return pl.pallas_call(
        functools.partial(_dispatch_kernel, N, E, n_rows // EXPERT_ROWS, _tile(N, DMA_CHUNK_TOKENS)),
        grid_spec=grid_spec,
        out_shape=jax.ShapeDtypeStruct((n_rows, D), F32),
        compiler_params=pltpu.CompilerParams(dimension_semantics=("arbitrary",),
                                             has_side_effects=True),
        name="moe_dispatch",
    )(dest_flat, zlo, zhi, n_used, h)


def _gather_kernel(n_tok, chunk, dest_ref, y_ref, o_ref, sem):
    def issue_token(t):
        for k in range(TOP_K):
            _row_copy(y_ref, o_ref, dest_ref[t * TOP_K + k], k * n_tok + t, sem).start()

    _chunked_token_copies(n_tok, chunk, issue_token,
                          lambda: _rows_wait(o_ref, chunk * TOP_K, sem))


def _gather_call(y_grouped, dest_flat, n_tok):
    D = y_grouped.shape[1]
    grid_spec = pltpu.PrefetchScalarGridSpec(
        num_scalar_prefetch=1,
        grid=(1,),
        in_specs=[pl.BlockSpec(memory_space=pl.ANY)],
        out_specs=pl.BlockSpec(memory_space=pl.ANY),
        scratch_shapes=[pltpu.SemaphoreType.DMA(())],
    )
    return pl.pallas_call(
        functools.partial(_gather_kernel, n_tok, _tile(n_tok, DMA_CHUNK_TOKENS)),
        grid_spec=grid_spec,
        out_shape=jax.ShapeDtypeStruct((TOP_K * n_tok, D), F32),
        compiler_params=pltpu.CompilerParams(dimension_semantics=("arbitrary",),
                                             has_side_effects=True),
        name="moe_gather",
    )(dest_flat, y_grouped)


def _expert_kernel(f_dim, be_ref, nu_ref, x_ref, wgu_ref, bgu_ref, wd_ref, bd_ref, o_ref):
    i = pl.program_id(0)

    @pl.when(i < nu_ref[0])
    def _():
        xb = x_ref[...].astype(BF16)
        gu = jnp.dot(xb, wgu_ref[0], preferred_element_type=F32) + bgu_ref[0]
        g = jnp.minimum(gu[:, :f_dim], SWIGLU_LIMIT)
        up = jnp.clip(gu[:, f_dim:], -SWIGLU_LIMIT, SWIGLU_LIMIT)
        act = g * jax.nn.sigmoid(SWIGLU_ALPHA * g) * (up + 1.0)
        o_ref[...] = jnp.dot(act.astype(BF16), wd_ref[0], preferred_element_type=F32) + bd_ref[0]

    @pl.when(i >= nu_ref[0])
    def _():
        o_ref[...] = jnp.zeros_like(o_ref)


def _expert_call(x_grouped, block_e, n_used, w_gu, b_gu, w_d, b_d):
    n_rows, D = x_grouped.shape
    E, _, F2 = w_gu.shape
    F = F2 // 2
    n_blocks = n_rows // EXPERT_ROWS
    grid_spec = pltpu.PrefetchScalarGridSpec(
        num_scalar_prefetch=2,
        grid=(n_blocks,),
        in_specs=[
            pl.BlockSpec((EXPERT_ROWS, D), lambda i, be, nu: (jnp.minimum(i, nu[0] - 1), 0)),
            pl.BlockSpec((1, D, F2), lambda i, be, nu: (be[i], 0, 0)),
            pl.BlockSpec((1, 1, F2), lambda i, be, nu: (be[i], 0, 0)),
            pl.BlockSpec((1, F, D), lambda i, be, nu: (be[i], 0, 0)),
            pl.BlockSpec((1, 1, D), lambda i, be, nu: (be[i], 0, 0)),
        ],
        out_specs=pl.BlockSpec((EXPERT_ROWS, D), lambda i, be, nu: (i, 0)),
    )
    return pl.pallas_call(
        functools.partial(_expert_kernel, F),
        grid_spec=grid_spec,
        out_shape=jax.ShapeDtypeStruct((n_rows, D), F32),
        compiler_params=_cparams(("arbitrary",), 56),
        name="experts",
    )(block_e, n_used, x_grouped, w_gu, b_gu.reshape(E, 1, F2), w_d, b_d.reshape(E, 1, D))


def _mixer_weights(D, w_mix_in, w_uq, w_ukv, fox_f_bias):
    Wb = D // 4
    H = Wb // HEAD_DIM
    q_lora = 3 * D // 16
    kv_lora = D // 8
    half = ROPE_DIM // 2
    o_cq = 3 * Wb
    o_ckv = o_cq + q_lora
    o_kr = o_ckv + kv_lora
    o_fox = o_kr + ROPE_DIM
    o_f = o_fox + 3 * Wb
    w_convfox = jnp.concatenate([w_mix_in[:, :o_cq], w_mix_in[:, o_fox:o_f]], axis=1).astype(BF16)
    w_cq = w_mix_in[:, o_cq:o_ckv].astype(BF16)
    w_ckv = w_mix_in[:, o_ckv:o_kr].astype(BF16)
    w_kr = w_mix_in[:, o_kr:o_fox]
    zpad = jnp.zeros((D, LANES - ROPE_DIM), F32)
    w_kra = jnp.concatenate([w_kr, zpad], axis=1).astype(BF16)
    w_krb = jnp.concatenate([-w_kr[:, half:], w_kr[:, :half], zpad], axis=1).astype(BF16)
    w_f = jnp.pad(w_mix_in[:, o_f:o_f + H], ((0, 0), (0, LANES - H))).astype(BF16)
    b_f = jnp.pad(fox_f_bias, (0, LANES - H)).reshape(1, LANES)

    uq = w_uq.reshape(q_lora, H, HEAD_DIM + ROPE_DIM)
    zq = jnp.zeros((q_lora, H, QK_PAD - HEAD_DIM - ROPE_DIM), F32)
    w_qa = jnp.concatenate([uq, zq], axis=2).reshape(q_lora, H * QK_PAD).astype(BF16)
    r1 = uq[:, :, HEAD_DIM:HEAD_DIM + half]
    r2 = uq[:, :, HEAD_DIM + half:]
    w_qb = jnp.concatenate([-r2, r1, zq], axis=2).reshape(q_lora, H * LANES).astype(BF16)

    ukv = w_ukv.reshape(kv_lora, H, 2 * HEAD_DIM)
    w_k = ukv[:, :, :HEAD_DIM].reshape(kv_lora, H * HEAD_DIM).astype(BF16)
    w_v = ukv[:, :, HEAD_DIM:].reshape(kv_lora, H * HEAD_DIM).astype(BF16)
    return dict(w_convfox=w_convfox, w_cq=w_cq, w_ckv=w_ckv, w_kra=w_kra, w_krb=w_krb,
                w_f=w_f, b_f=b_f, w_qa=w_qa, w_qb=w_qb, w_k=w_k, w_v=w_v)


def _mixer(h, B, S, D, rope_cos, rope_sin, mw, conv_w, q_norm_g, kv_norm_g,
           w_branch, w_merge_gate, b_merge_gate, w_o):
    Wb = D // 4
    H = Wb // HEAD_DIM
    u = _mm(h, mw["w_convfox"], BF16, tn_pref=512, name="mix_in")
    o_a = _conv_call(u, conv_w, B, S, Wb)

    q_mla = _mla_q_call(h, mw["w_cq"], q_norm_g, mw["w_qa"], mw["w_qb"], rope_cos, rope_sin, H)
    k_mla, v_mla = _mla_kv_call(h, mw["w_ckv"], kv_norm_g, mw["w_k"], mw["w_v"],
                                mw["w_kra"], mw["w_krb"], rope_cos, rope_sin, H)
    o_b = _flash_call(q_mla, k_mla, v_mla, B, S, H, (HEAD_DIM + ROPE_DIM) ** -0.5, name="flash_mla")

    fox_scale = HEAD_DIM ** -0.5
    q_fox, k_fox = _fox_prep_call(h, u, mw["w_f"], mw["b_f"], B, S, H, 3, 4, fox_scale)
    o_c = _flash_call(q_fox, k_fox, u, B, S, H, fox_scale, v_blk0=5 * H, name="flash_fox")

    merged = _merge_call(h, w_merge_gate.astype(BF16), b_merge_gate, (o_a, o_b, o_c),
                         w_branch.astype(BF16))
    return _mm(merged, w_o.astype(BF16), F32, name="mix_out")


def _moe(x, g_pre, scale, shift, g_post, gate, w_router, b_router, w_gate_up, b_gate_up,
         w_down, b_down):
    B, S, D = x.shape
    N = B * S
    E = w_router.shape[1]
    h, top_e, top_p = _norm_router_call(x, g_pre, scale, shift, w_router, b_router)
    rank, counts8 = _rank_call(top_e)

    counts = counts8[0, :E]
    padded = (counts + EXPERT_ROWS - 1) // EXPERT_ROWS * EXPERT_ROWS
    pad_end = jnp.cumsum(padded)
    pad_start = pad_end - padded
    e_flat = top_e[:, :TOP_K].reshape(-1)
    dest = pad_start[e_flat] + rank[:, :TOP_K].reshape(-1)
    n_blocks = (N * TOP_K + E * (EXPERT_ROWS - 1) + EXPERT_ROWS - 1) // EXPERT_ROWS
    n_rows = n_blocks * EXPERT_ROWS
    block_start = jnp.arange(n_blocks, dtype=jnp.int32) * EXPERT_ROWS
    block_e = jnp.minimum(
        jnp.sum((pad_end[None, :] <= block_start[:, None]).astype(jnp.int32), axis=1), E - 1)
    n_used = (pad_end[-1:] // EXPERT_ROWS).astype(jnp.int32)

    x_grouped = _dispatch_call(h.reshape(N, D), dest.astype(jnp.int32),
                               (pad_start + counts).astype(jnp.int32),
                               pad_end.astype(jnp.int32), n_used, n_rows)
    y_grouped = _expert_call(x_grouped, block_e, n_used, w_gate_up.astype(BF16), b_gate_up,
                             w_down.astype(BF16), b_down)
    y_tok = _gather_call(y_grouped, dest.astype(jnp.int32), N).reshape(TOP_K, N, D)
    return _combine_post_call(x, y_tok, top_p, g_post, gate)


def kernel(x, c, positions, w_ada, b_ada, g_mix_pre, g_mix_post, g_ffn_pre, g_ffn_post, w_mix_in, conv_w, q_norm_g, kv_norm_g, w_uq, w_ukv, fox_f_bias, w_branch, w_merge_gate, b_merge_gate, w_o, w_router, b_router, w_gate_up, b_gate_up, w_down, b_down):
    B, S, D = x.shape
    L = w_ada.shape[0]
    N = B * S

    c8 = jnp.pad(c, ((0, 8 - B), (0, 0)))
    ada = _ada(c8, w_ada, b_ada)[:, :B].reshape(L, B, N_ADA, 1, D)

    half = ROPE_DIM // 2
    inv_freq = ROPE_THETA ** (-jnp.arange(half, dtype=F32) / half)
    ang = positions.astype(F32).reshape(N, 1) * inv_freq
    zer = jnp.zeros((N, LANES - ROPE_DIM), F32)
    rope_cos = jnp.concatenate([jnp.cos(ang), jnp.cos(ang), zer], axis=1)
    rope_sin = jnp.concatenate([jnp.sin(ang), jnp.sin(ang), zer], axis=1)

    for l in range(L):
        shift_m, scale_m, gate_m = ada[l, :, 0], ada[l, :, 1], ada[l, :, 2]
        shift_f, scale_f, gate_f = ada[l, :, 3], ada[l, :, 4], ada[l, :, 5]

        h = _norm_mod_call(x, g_mix_pre[l], scale_m, shift_m).reshape(N, D)
        mw = _mixer_weights(D, w_mix_in[l], w_uq[l], w_ukv[l], fox_f_bias[l])
        y = _mixer(h, B, S, D, rope_cos, rope_sin, mw, conv_w[l], q_norm_g[l], kv_norm_g[l],
                   w_branch[l], w_merge_gate[l], b_merge_gate[l], w_o[l])
        x = _post_call(x, y, g_mix_post[l], gate_m)

        x = _moe(x, g_ffn_pre[l], scale_f, shift_f, g_ffn_post[l], gate_f, w_router[l],
                 b_router[l], w_gate_up[l], b_gate_up[l], w_down[l], b_down[l])
    return x
```

```python
import functools

import jax
import jax.numpy as jnp
from jax import lax
from jax.experimental import pallas as pl
from jax.experimental.pallas import tpu as pltpu

NORM_EPS = 1e-6
ROPE_THETA = 10000.0
HEAD_DIM = 128
ROPE_DIM = 64
CONV_K = 3
TOP_K = 4
SWIGLU_ALPHA = 1.702
SWIGLU_LIMIT = 7.0
N_ADA = 6
LANES = 128
QK_PAD = 2 * LANES
EXPERT_ROWS = 256
MIB = 1024 * 1024

F32 = jnp.float32
BF16 = jnp.bfloat16


def _cparams(sem, vmem_mib=48):
    return pltpu.CompilerParams(dimension_semantics=sem, vmem_limit_bytes=vmem_mib * MIB)


def _tile(n, pref):
    t = min(pref, n)
    while n % t:
        t //= 2
    return t


def _split3(x):
    hi = x.astype(BF16)
    r1 = x - hi.astype(F32)
    mid = r1.astype(BF16)
    lo = (r1 - mid.astype(F32)).astype(BF16)
    return hi, mid, lo


def _ada_kernel(c_ref, w_ref, b_ref, o_ref):
    c = c_ref[...]
    c_act = (c * jax.nn.sigmoid(c)).astype(BF16)
    acc = jnp.dot(c_act, w_ref[0].astype(BF16), preferred_element_type=F32)
    o_ref[0] = acc + b_ref[0]


def _ada(c8, w_ada, b_ada):
    L, D, N = w_ada.shape
    tn = _tile(N, 1024)
    return pl.pallas_call(
        _ada_kernel,
        grid=(L, N // tn),
        in_specs=[
            pl.BlockSpec((8, D), lambda l, j: (0, 0)),
            pl.BlockSpec((1, D, tn), lambda l, j: (l, 0, j)),
            pl.BlockSpec((1, 1, tn), lambda l, j: (l, 0, j)),
        ],
        out_specs=pl.BlockSpec((1, 8, tn), lambda l, j: (l, 0, j)),
        out_shape=jax.ShapeDtypeStruct((L, 8, N), F32),
        compiler_params=_cparams(("parallel", "parallel"), 56),
        name="ada",
    )(c8, w_ada, b_ada.reshape(L, 1, N))


def _norm_mod(x, g, scale, shift):
    xf = x
    y = xf * lax.rsqrt(jnp.mean(xf * xf, axis=-1, keepdims=True) + NORM_EPS)
    return (y * g) * (1.0 + scale) + shift


def _norm_mod_kernel(x_ref, g_ref, sc_ref, sh_ref, o_ref):
    o_ref[0] = _norm_mod(x_ref[0], g_ref[...], sc_ref[0], sh_ref[0]).astype(o_ref.dtype)


def _norm_mod_call(x, g, scale, shift):
    B, S, D = x.shape
    ts = _tile(S, 256)
    return pl.pallas_call(
        _norm_mod_kernel,
        grid=(B, S // ts),
        in_specs=[
            pl.BlockSpec((1, ts, D), lambda b, i: (b, i, 0)),
            pl.BlockSpec((1, D), lambda b, i: (0, 0)),
            pl.BlockSpec((1, 1, D), lambda b, i: (b, 0, 0)),
            pl.BlockSpec((1, 1, D), lambda b, i: (b, 0, 0)),
        ],
        out_specs=pl.BlockSpec((1, ts, D), lambda b, i: (b, i, 0)),
        out_shape=jax.ShapeDtypeStruct((B, S, D), BF16),
        compiler_params=_cparams(("parallel", "parallel")),
        name="norm_mod",
    )(x, g.reshape(1, D), scale, shift)


def _norm_router_kernel(n_experts, x_ref, g_ref, sc_ref, sh_ref, wr_hi_ref, wr_lo_ref, br_ref,
                        h_ref, e_ref, p_ref):
    h = _norm_mod(x_ref[0], g_ref[...], sc_ref[0], sh_ref[0])
    h_ref[0] = h
    h_hi = h.astype(BF16)
    h_lo = (h - h_hi.astype(F32)).astype(BF16)
    logits = (jnp.dot(h_hi, wr_hi_ref[...], preferred_element_type=F32)
              + jnp.dot(h_hi, wr_lo_ref[...], preferred_element_type=F32)
              + jnp.dot(h_lo, wr_hi_ref[...], preferred_element_type=F32)) + br_ref[...]
    lane = lax.broadcasted_iota(jnp.int32, logits.shape, 1)
    vals = jnp.where(lane < n_experts, logits, -jnp.inf)
    tops, idxs = [], []
    for _ in range(TOP_K):
        m = jnp.max(vals, axis=-1, keepdims=True)
        idx = jnp.min(jnp.where(vals == m, lane, LANES), axis=-1, keepdims=True)
        tops.append(m)
        idxs.append(idx)
        vals = jnp.where(lane == idx, -jnp.inf, vals)
    exps = [jnp.exp(t - tops[0]) for t in tops]
    denom = exps[0]
    for e in exps[1:]:
        denom = denom + e
    e_out = jnp.zeros(logits.shape, jnp.int32)
    p_out = jnp.zeros(logits.shape, F32)
    for k in range(TOP_K):
        e_out = jnp.where(lane == k, idxs[k], e_out)
        p_out = jnp.where(lane == k, exps[k] / denom, p_out)
    e_ref[...] = e_out
    p_ref[...] = p_out


def _norm_router_call(x, g, scale, shift, w_router, b_router):
    B, S, D = x.shape
    E = w_router.shape[1]
    ts = _tile(S, 256)
    nb = S // ts
    wr = jnp.pad(w_router, ((0, 0), (0, LANES - E)))
    wr_hi = wr.astype(BF16)
    wr_lo = (wr - wr_hi.astype(F32)).astype(BF16)
    br = jnp.pad(b_router, (0, LANES - E)).reshape(1, LANES)
    return pl.pallas_call(
        functools.partial(_norm_router_kernel, E),
        grid=(B, nb),
        in_specs=[
            pl.BlockSpec((1, ts, D), lambda b, i: (b, i, 0)),
            pl.BlockSpec((1, D), lambda b, i: (0, 0)),
            pl.BlockSpec((1, 1, D), lambda b, i: (b, 0, 0)),
            pl.BlockSpec((1, 1, D), lambda b, i: (b, 0, 0)),
            pl.BlockSpec((D, LANES), lambda b, i: (0, 0)),
            pl.BlockSpec((D, LANES), lambda b, i: (0, 0)),
            pl.BlockSpec((1, LANES), lambda b, i: (0, 0)),
        ],
        out_specs=[
            pl.BlockSpec((1, ts, D), lambda b, i: (b, i, 0)),
            pl.BlockSpec((ts, LANES), lambda b, i: (b * nb + i, 0)),
            pl.BlockSpec((ts, LANES), lambda b, i: (b * nb + i, 0)),
        ],
        out_shape=[
            jax.ShapeDtypeStruct((B, S, D), F32),
            jax.ShapeDtypeStruct((B * S, LANES), jnp.int32),
            jax.ShapeDtypeStruct((B * S, LANES), F32),
        ],
        compiler_params=_cparams(("parallel", "parallel")),
        name="norm_router",
    )(x, g.reshape(1, D), scale, shift, wr_hi, wr_lo, br)


def _mm_kernel(a_ref, w_ref, o_ref):
    o_ref[...] = jnp.dot(a_ref[...], w_ref[...], preferred_element_type=F32).astype(o_ref.dtype)


def _mm(a, w, out_dtype, tm_pref=1024, tn_pref=512, name="mm"):
    M, K = a.shape
    N = w.shape[1]
    tm = _tile(M, tm_pref)
    tn = _tile(N, tn_pref)
    return pl.pallas_call(
        _mm_kernel,
        grid=(M // tm, N // tn),
        in_specs=[
            pl.BlockSpec((tm, K), lambda i, j: (i, 0)),
            pl.BlockSpec((K, tn), lambda i, j: (0, j)),
        ],
        out_specs=pl.BlockSpec((tm, tn), lambda i, j: (i, j)),
        out_shape=jax.ShapeDtypeStruct((M, N), out_dtype),
        compiler_params=_cparams(("parallel", "parallel"), 56),
        name=name,
    )(a, w)


def _conv_kernel(b_ref, c_ref, h_ref, w_ref, o_ref):
    g = c_ref[...].astype(F32) * h_ref[...].astype(F32)
    row = lax.broadcasted_iota(jnp.int32, g.shape, 0)
    g1 = jnp.where(row >= 1, pltpu.roll(g, 1, 0), 0.0)
    g2 = jnp.where(row >= 2, pltpu.roll(g, 2, 0), 0.0)
    w = w_ref[...]
    conv = w[0:1] * g2 + w[1:2] * g1 + w[2:3] * g
    o_ref[...] = (b_ref[...].astype(F32) * conv).astype(o_ref.dtype)


def _conv_call(u, conv_w, B, S, C):
    tc = _tile(C, 256)
    nc = C // tc
    w8 = jnp.pad(conv_w, ((0, 8 - CONV_K), (0, 0)))
    return pl.pallas_call(
        _conv_kernel,
        grid=(B, nc),
        in_specs=[
            pl.BlockSpec((S, tc), lambda b, j: (b, j)),
            pl.BlockSpec((S, tc), lambda b, j: (b, nc + j)),
            pl.BlockSpec((S, tc), lambda b, j: (b, 2 * nc + j)),
            pl.BlockSpec((8, tc), lambda b, j: (0, j)),
        ],
        out_specs=pl.BlockSpec((S, tc), lambda b, j: (b, j)),
        out_shape=jax.ShapeDtypeStruct((B * S, C), BF16),
        compiler_params=_cparams(("parallel", "parallel")),
        name="gated_conv",
    )(u, u, u, w8)


def _rms(x, g):
    return x * lax.rsqrt(jnp.mean(x * x, axis=-1, keepdims=True) + NORM_EPS) * g


def _mla_q_kernel(n_heads, h_ref, wc_ref, g_ref, wa_ref, wb_ref, cos_ref, sin_ref, q_ref):
    c_q = jnp.dot(h_ref[...], wc_ref[...], preferred_element_type=F32)
    n = _rms(c_q, g_ref[...]).astype(BF16)
    a = jnp.dot(n, wa_ref[...], preferred_element_type=F32)
    bm = jnp.dot(n, wb_ref[...], preferred_element_type=F32)
    cos = cos_ref[...]
    sin = sin_ref[...]
    for hd in range(n_heads):
        o = hd * QK_PAD
        q_ref[:, o:o + LANES] = a[:, o:o + LANES].astype(q_ref.dtype)
        rot = a[:, o + LANES:o + QK_PAD] * cos + bm[:, hd * LANES:(hd + 1) * LANES] * sin
        q_ref[:, o + LANES:o + QK_PAD] = rot.astype(q_ref.dtype)


def _mla_q_call(h, w_cq, q_norm_g, w_a, w_b, cos_t, sin_t, n_heads):
    M, D = h.shape
    R = w_cq.shape[1]
    tm = _tile(M, 512)
    const = lambda i: (0, 0)
    return pl.pallas_call(
        functools.partial(_mla_q_kernel, n_heads),
        grid=(M // tm,),
        in_specs=[
            pl.BlockSpec((tm, D), lambda i: (i, 0)),
            pl.BlockSpec((D, R), const),
            pl.BlockSpec((1, R), const),
            pl.BlockSpec((R, n_heads * QK_PAD), const),
            pl.BlockSpec((R, n_heads * LANES), const),
            pl.BlockSpec((tm, LANES), lambda i: (i, 0)),
            pl.BlockSpec((tm, LANES), lambda i: (i, 0)),
        ],
        out_specs=pl.BlockSpec((tm, n_heads * QK_PAD), lambda i: (i, 0)),
        out_shape=jax.ShapeDtypeStruct((M, n_heads * QK_PAD), BF16),
        compiler_params=_cparams(("parallel",), 56),
        name="mla_q",
    )(h, w_cq, q_norm_g.reshape(1, R), w_a, w_b, cos_t, sin_t)


def _mla_kv_kernel(n_heads, h_ref, wc_ref, g_ref, wk_ref, wv_ref, wra_ref, wrb_ref,
                   cos_ref, sin_ref, k_ref, v_ref):
    hb = h_ref[...]
    c_kv = jnp.dot(hb, wc_ref[...], preferred_element_type=F32)
    n = _rms(c_kv, g_ref[...]).astype(BF16)
    k_nope = jnp.dot(n, wk_ref[...], preferred_element_type=F32)
    v_ref[...] = jnp.dot(n, wv_ref[...], preferred_element_type=F32).astype(v_ref.dtype)
    ra = jnp.dot(hb, wra_ref[...], preferred_element_type=F32)
    rb = jnp.dot(hb, wrb_ref[...], preferred_element_type=F32)
    k_rope = (ra * cos_ref[...] + rb * sin_ref[...]).astype(k_ref.dtype)
    for hd in range(n_heads):
        o = hd * QK_PAD
        k_ref[:, o:o + LANES] = k_nope[:, hd * LANES:(hd + 1) * LANES].astype(k_ref.dtype)
        k_ref[:, o + LANES:o + QK_PAD] = k_rope


def _mla_kv_call(h, w_ckv, kv_norm_g, w_k, w_v, w_ra, w_rb, cos_t, sin_t, n_heads):
    M, D = h.shape
    R = w_ckv.shape[1]
    tm = _tile(M, 512)
    const = lambda i: (0, 0)
    return pl.pallas_call(
        functools.partial(_mla_kv_kernel, n_heads),
        grid=(M // tm,),
        in_specs=[
            pl.BlockSpec((tm, D), lambda i: (i, 0)),
            pl.BlockSpec((D, R), const),
            pl.BlockSpec((1, R), const),
            pl.BlockSpec((R, n_heads * LANES), const),
            pl.BlockSpec((R, n_heads * LANES), const),
            pl.BlockSpec((D, LANES), const),
            pl.BlockSpec((D, LANES), const),
            pl.BlockSpec((tm, LANES), lambda i: (i, 0)),
            pl.BlockSpec((tm, LANES), lambda i: (i, 0)),
        ],
        out_specs=[
            pl.BlockSpec((tm, n_heads * QK_PAD), lambda i: (i, 0)),
            pl.BlockSpec((tm, n_heads * LANES), lambda i: (i, 0)),
        ],
        out_shape=[
            jax.ShapeDtypeStruct((M, n_heads * QK_PAD), BF16),
            jax.ShapeDtypeStruct((M, n_heads * LANES), BF16),
        ],
        compiler_params=_cparams(("parallel",), 56),
        name="mla_kv",
    )(h, w_ckv, kv_norm_g.reshape(1, R), w_k, w_v, w_ra, w_rb, cos_t, sin_t)


def _fox_prep_kernel(n_heads, inv_scale, h_ref, wf_ref, bf_ref, q_ref, k_ref,
                     qo_ref, ko_ref, carry_ref):
    @pl.when(pl.program_id(1) == 0)
    def _():
        carry_ref[...] = jnp.zeros_like(carry_ref)

    f = jnp.dot(h_ref[...], wf_ref[...], preferred_element_type=F32) + bf_ref[...]
    log_f = jax.nn.log_sigmoid(f)
    ts = log_f.shape[0]
    row = lax.broadcasted_iota(jnp.int32, (ts, ts), 0)
    col = lax.broadcasted_iota(jnp.int32, (ts, ts), 1)
    tri = jnp.where(col <= row, 1.0, 0.0).astype(BF16)
    cum = carry_ref[0:1, :]
    for part in _split3(log_f):
        cum = cum + jnp.dot(tri, part, preferred_element_type=F32)
    carry_ref[...] = jnp.broadcast_to(cum[ts - 1:ts, :], carry_ref.shape)

    lane = lax.broadcasted_iota(jnp.int32, (ts, LANES), 1)
    cum_s = cum * inv_scale
    for hd in range(n_heads):
        colv = jnp.broadcast_to(cum_s[:, hd:hd + 1], (ts, LANES))
        hi, mid, lo = (part.astype(F32) for part in _split3(colv))
        q_aug = jnp.where(lane == 0, hi, jnp.where(lane == 1, mid, jnp.where(
            lane == 2, lo, jnp.where(lane < 6, 1.0, 0.0))))
        k_aug = jnp.where(lane < 3, 1.0, jnp.where(lane == 3, -hi, jnp.where(
            lane == 4, -mid, jnp.where(lane == 5, -lo, 0.0))))
        o = hd * QK_PAD
        qo_ref[:, o:o + LANES] = q_ref[:, hd * LANES:(hd + 1) * LANES]
        qo_ref[:, o + LANES:o + QK_PAD] = q_aug.astype(BF16)
        ko_ref[:, o:o + LANES] = k_ref[:, hd * LANES:(hd + 1) * LANES]
        ko_ref[:, o + LANES:o + QK_PAD] = k_aug.astype(BF16)


def _fox_prep_call(h, u, w_f, b_f, B, S, n_heads, q_blk, k_blk, scale):
    M, D = h.shape
    W = n_heads * LANES
    ts = _tile(S, 512)
    nb = S // ts
    row = lambda b, i: (b * nb + i, 0)
    return pl.pallas_call(
        functools.partial(_fox_prep_kernel, n_heads, 1.0 / scale),
        grid=(B, nb),
        in_specs=[
            pl.BlockSpec((ts, D), row),
            pl.BlockSpec((D, LANES), lambda b, i: (0, 0)),
            pl.BlockSpec((1, LANES), lambda b, i: (0, 0)),
            pl.BlockSpec((ts, W), lambda b, i: (b * nb + i, q_blk)),
            pl.BlockSpec((ts, W), lambda b, i: (b * nb + i, k_blk)),
        ],
        out_specs=[
            pl.BlockSpec((ts, n_heads * QK_PAD), row),
            pl.BlockSpec((ts, n_heads * QK_PAD), row),
        ],
        out_shape=[
            jax.ShapeDtypeStruct((M, n_heads * QK_PAD), BF16),
            jax.ShapeDtypeStruct((M, n_heads * QK_PAD), BF16),
        ],
        scratch_shapes=[pltpu.VMEM((8, LANES), F32)],
        compiler_params=_cparams(("parallel", "arbitrary")),
        name="fox_prep",
    )(h, w_f, b_f, u, u)


def _flash_kernel(scale_log2, t, hp, q_ref, k_ref, v_ref, o_ref, m_ref, l_ref, acc_ref):
    i = pl.program_id(2)
    m_ref[...] = jnp.full_like(m_ref, -jnp.inf)
    l_ref[...] = jnp.zeros_like(l_ref)
    acc_ref[...] = jnp.zeros_like(acc_ref)

    def tile(j, masked):
        off = pl.multiple_of(j * t, t)
        for hh in range(hp):
            q = q_ref[:, hh * QK_PAD:(hh + 1) * QK_PAD]
            k = k_ref[pl.ds(off, t), hh * QK_PAD:(hh + 1) * QK_PAD]
            v = v_ref[pl.ds(off, t), hh * HEAD_DIM:(hh + 1) * HEAD_DIM]
            s = lax.dot_general(q, k, (((1,), (1,)), ((), ())),
                                preferred_element_type=F32) * scale_log2
            if masked:
                row = lax.broadcasted_iota(jnp.int32, s.shape, 0)
                col = lax.broadcasted_iota(jnp.int32, s.shape, 1)
                s = jnp.where(col <= row, s, -jnp.inf)
            parts = [s[:, c * LANES:(c + 1) * LANES] for c in range(t // LANES)]
            part_max = parts[0]
            for part in parts[1:]:
                part_max = jnp.maximum(part_max, part)
            m_prev = m_ref[hh]
            m_new = jnp.maximum(m_prev, jnp.max(part_max, axis=-1, keepdims=True))
            alpha = jnp.exp2(m_prev - m_new)
            probs = [jnp.exp2(part - m_new) for part in parts]
            l_part = probs[0]
            for pr in probs[1:]:
                l_part = l_part + pr
            l_ref[hh] = alpha * l_ref[hh] + l_part
            p = jnp.concatenate(probs, axis=1).astype(v.dtype)
            acc_ref[hh] = alpha * acc_ref[hh] + jnp.dot(p, v, preferred_element_type=F32)
            m_ref[hh] = m_new

    def body(j, c):
        tile(j, False)
        return c

    lax.fori_loop(0, i, body, 0)
    tile(i, True)
    for hh in range(hp):
        l_row = jnp.sum(l_ref[hh], axis=-1, keepdims=True)
        o_ref[:, hh * HEAD_DIM:(hh + 1) * HEAD_DIM] = (acc_ref[hh] / l_row).astype(o_ref.dtype)


def _flash_call(q, k, v, B, S, n_heads, scale, v_blk0=0, name="flash"):
    M = q.shape[0]
    t = _tile(S, 512)
    nq = S // t
    hp = 4
    assert n_heads % hp == 0 and v_blk0 % hp == 0
    return pl.pallas_call(
        functools.partial(_flash_kernel, scale * 1.4426950408889634, t, hp),
        grid=(B, n_heads // hp, nq),
        in_specs=[
            pl.BlockSpec((t, hp * QK_PAD), lambda b, h, i: (b * nq + i, h)),
            pl.BlockSpec((S, hp * QK_PAD), lambda b, h, i: (b, h)),
            pl.BlockSpec((S, hp * HEAD_DIM), lambda b, h, i: (b, v_blk0 // hp + h)),
        ],
        out_specs=pl.BlockSpec((t, hp * HEAD_DIM), lambda b, h, i: (b * nq + i, h)),
        out_shape=jax.ShapeDtypeStruct((M, n_heads * HEAD_DIM), BF16),
        scratch_shapes=[
            pltpu.VMEM((hp, t, LANES), F32),
            pltpu.VMEM((hp, t, LANES), F32),
            pltpu.VMEM((hp, t, HEAD_DIM), F32),
        ],
        compiler_params=_cparams(("parallel", "parallel", "arbitrary")),
        name=name,
    )(q, k, v)


def _merge_kernel(n_br, h_ref, wg_ref, bg_ref, *rest):
    br_refs = rest[:n_br]
    wb_ref = rest[n_br]
    o_ref = rest[n_br + 1]
    hb = h_ref[...]
    merged = None
    for n in range(n_br):
        gate = jax.nn.sigmoid(jnp.dot(hb, wg_ref[n], preferred_element_type=F32) + bg_ref[n])
        proj = jnp.dot(br_refs[n][...], wb_ref[n], preferred_element_type=F32)
        merged = gate * proj if merged is None else merged + gate * proj
    o_ref[...] = merged.astype(o_ref.dtype)


def _merge_call(h, w_gate, b_gate, branches, w_branch):
    M, D = h.shape
    n_br, W, _ = w_branch.shape
    tm = _tile(M, 512)
    tn = _tile(D, 256)
    in_specs = [
        pl.BlockSpec((tm, D), lambda i, j: (i, 0)),
        pl.BlockSpec((n_br, D, tn), lambda i, j: (0, 0, j)),
        pl.BlockSpec((n_br, 1, tn), lambda i, j: (0, 0, j)),
    ]
    in_specs += [pl.BlockSpec((tm, W), lambda i, j: (i, 0)) for _ in range(n_br)]
    in_specs += [pl.BlockSpec((n_br, W, tn), lambda i, j: (0, 0, j))]
    return pl.pallas_call(
        functools.partial(_merge_kernel, n_br),
        grid=(M // tm, D // tn),
        in_specs=in_specs,
        out_specs=pl.BlockSpec((tm, tn), lambda i, j: (i, j)),
        out_shape=jax.ShapeDtypeStruct((M, D), BF16),
        compiler_params=_cparams(("parallel", "parallel"), 56),
        name="gated_merge",
    )(h, w_gate, b_gate.reshape(n_br, 1, D), *branches, w_branch)


def _post_kernel(x_ref, y_ref, g_ref, gate_ref, o_ref):
    o_ref[0] = x_ref[0] + gate_ref[0] * _rms(y_ref[0], g_ref[...])


def _post_call(x, y, g, gate):
    B, S, D = x.shape
    ts = _tile(S, 256)
    blk = pl.BlockSpec((1, ts, D), lambda b, i: (b, i, 0))
    return pl.pallas_call(
        _post_kernel,
        grid=(B, S // ts),
        in_specs=[blk, blk, pl.BlockSpec((1, D), lambda b, i: (0, 0)),
                  pl.BlockSpec((1, 1, D), lambda b, i: (b, 0, 0))],
        out_specs=blk,
        out_shape=jax.ShapeDtypeStruct((B, S, D), F32),
        compiler_params=_cparams(("parallel", "parallel")),
        name="post_residual",
    )(x, y.reshape(B, S, D), g.reshape(1, D), gate)


def _combine_post_kernel(ts, n_steps, dest_ref, x_ref, y_hbm, p_ref, g_ref, gate_ref, o_ref,
                         ybuf, sems):
    s = pl.program_id(0)

    def issue(step, slot):
        def tok_body(r, c):
            t = step * ts + r
            for k in range(TOP_K):
                pltpu.make_async_copy(y_hbm.at[pl.ds(dest_ref[t * TOP_K + k], 1)],
                                      ybuf.at[slot, k, pl.ds(r, 1)], sems.at[slot]).start()
            return c
        lax.fori_loop(0, ts, tok_body, 0)

    @pl.when(s == 0)
    def _():
        issue(0, 0)

    @pl.when(s + 1 < n_steps)
    def _():
        issue(s + 1, (s + 1) % 2)

    slot = s % 2
    pltpu.make_async_copy(ybuf.at[slot], ybuf.at[slot], sems.at[slot]).wait()
    p = p_ref[...]
    y = ybuf[slot, 0] * p[:, 0:1]
    for k in range(1, TOP_K):
        y = y + ybuf[slot, k] * p[:, k:k + 1]
    o_ref[0] = x_ref[0] + gate_ref[0] * _rms(y, g_ref[...])


def _combine_post_call(x, y_grouped, dest_flat, top_p, g, gate):
    B, S, D = x.shape
    ts = _tile(S, 128)
    nb = S // ts
    blk = pl.BlockSpec((1, ts, D), lambda s, d: (s // nb, s % nb, 0))
    grid_spec = pltpu.PrefetchScalarGridSpec(
        num_scalar_prefetch=1,
        grid=(B * nb,),
        in_specs=[blk,
                  pl.BlockSpec(memory_space=pl.ANY),
                  pl.BlockSpec((ts, LANES), lambda s, d: (s, 0)),
                  pl.BlockSpec((1, D), lambda s, d: (0, 0)),
                  pl.BlockSpec((1, 1, D), lambda s, d: (s // nb, 0, 0))],
        out_specs=blk,
        scratch_shapes=[pltpu.VMEM((2, TOP_K, ts, D), F32), pltpu.SemaphoreType.DMA((2,))],
    )
    return pl.pallas_call(
        functools.partial(_combine_post_kernel, ts, B * nb),
        grid_spec=grid_spec,
        out_shape=jax.ShapeDtypeStruct((B, S, D), F32),
        compiler_params=_cparams(("arbitrary",)),
        name="combine_post",
    )(dest_flat, x, y_grouped, top_p, g.reshape(1, D), gate)


def _rank_kernel(e_ref, rank_ref, cnt_ref, carry_ref):
    @pl.when(pl.program_id(0) == 0)
    def _():
        carry_ref[...] = jnp.zeros_like(carry_ref)

    e = e_ref[...]
    tb = e.shape[0]
    lane = lax.broadcasted_iota(jnp.int32, e.shape, 1)
    sel = [jnp.broadcast_to(e[:, k:k + 1], e.shape) == lane for k in range(TOP_K)]
    onehot = jnp.zeros(e.shape, F32)
    for s in sel:
        onehot = onehot + jnp.where(s, 1.0, 0.0)
    row = lax.broadcasted_iota(jnp.int32, (tb, tb), 0)
    col = lax.broadcasted_iota(jnp.int32, (tb, tb), 1)
    strict = jnp.where(col < row, 1.0, 0.0).astype(BF16)
    before = carry_ref[0:1, :] + jnp.dot(strict, onehot.astype(BF16), preferred_element_type=F32)
    rank = jnp.zeros(e.shape, F32)
    for k in range(TOP_K):
        r_k = jnp.sum(jnp.where(sel[k], before, 0.0), axis=-1, keepdims=True)
        rank = jnp.where(lane == k, r_k, rank)
    rank_ref[...] = rank.astype(jnp.int32)
    total = carry_ref[0:1, :] + jnp.sum(onehot, axis=0, keepdims=True)
    carry_ref[...] = jnp.broadcast_to(total, carry_ref.shape)
    cnt_ref[...] = jnp.broadcast_to(total, cnt_ref.shape).astype(jnp.int32)


def _rank_call(top_e):
    N = top_e.shape[0]
    tb = _tile(N, 512)
    return pl.pallas_call(
        _rank_kernel,
        grid=(N // tb,),
        in_specs=[pl.BlockSpec((tb, LANES), lambda i: (i, 0))],
        out_specs=[pl.BlockSpec((tb, LANES), lambda i: (i, 0)),
                   pl.BlockSpec((8, LANES), lambda i: (0, 0))],
        out_shape=[jax.ShapeDtypeStruct((N, LANES), jnp.int32),
                   jax.ShapeDtypeStruct((8, LANES), jnp.int32)],
        scratch_shapes=[pltpu.VMEM((8, LANES), F32)],
        compiler_params=_cparams(("arbitrary",)),
        name="route_rank",
    )(top_e)


def _row_copy(src, dst, s_row, d_row, sem):
    return pltpu.make_async_copy(src.at[pl.ds(s_row, 1)], dst.at[pl.ds(d_row, 1)], sem)


def _rows_wait(ref, n, sem):
    pltpu.make_async_copy(ref.at[pl.ds(0, n)], ref.at[pl.ds(0, n)], sem).wait()


def _dispatch_kernel(tb, n_experts, n_blocks, dest_ref, zlo_ref, zhi_ref, nblk_ref,
                     h_ref, xg_ref, zero_ref, sem):
    i = pl.program_id(0)

    @pl.when(i == 0)
    def _():
        zero_ref[...] = jnp.zeros_like(zero_ref)

        def exp_body(e, carry):
            def issue(r, c2):
                _row_copy(zero_ref, xg_ref, 0, r, sem).start()
                return c2

            def wait(r, c2):
                _row_copy(zero_ref, xg_ref, 0, r, sem).wait()
                return c2

            lax.fori_loop(zlo_ref[e], zhi_ref[e], issue, 0)
            lax.fori_loop(zlo_ref[e], zhi_ref[e], wait, 0)
            return carry

        lax.fori_loop(0, n_experts, exp_body, 0)

        def tail_body(b, carry):
            cp = pltpu.make_async_copy(
                zero_ref,
                xg_ref.at[pl.ds(pl.multiple_of(b * EXPERT_ROWS, EXPERT_ROWS), EXPERT_ROWS)], sem)
            cp.start()
            cp.wait()
            return carry

        lax.fori_loop(nblk_ref[0], n_blocks, tail_body, 0)

    def tok_body(r, carry):
        t = i * tb + r
        for k in range(TOP_K):
            _row_copy(h_ref, xg_ref, r, dest_ref[t * TOP_K + k], sem).start()
        return carry

    lax.fori_loop(0, tb, tok_body, 0)
    _rows_wait(xg_ref, tb * TOP_K, sem)


def _dispatch_call(h, dest_flat, zlo, zhi, n_used, n_rows):
    N, D = h.shape
    E = zlo.shape[0]
    tb = _tile(N, 256)
    grid_spec = pltpu.PrefetchScalarGridSpec(
        num_scalar_prefetch=4,
        grid=(N // tb,),
        in_specs=[pl.BlockSpec((tb, D), lambda i, *_: (i, 0))],
        out_specs=pl.BlockSpec(memory_space=pl.ANY),
        scratch_shapes=[pltpu.VMEM((EXPERT_ROWS, D), F32), pltpu.SemaphoreType.DMA(())],
    )
    return pl.pallas_call(
        functools.partial(_dispatch_kernel, tb, E, n_rows // EXPERT_ROWS),
        grid_spec=grid_spec,
        out_shape=jax.ShapeDtypeStruct((n_rows, D), F32),
        compiler_params=pltpu.CompilerParams(dimension_semantics=("arbitrary",),
                                             has_side_effects=True),
        name="moe_dispatch",
    )(dest_flat, zlo, zhi, n_used, h)


def _expert_kernel(f_dim, be_ref, nu_ref, x_ref, wgu_ref, bgu_ref, wd_ref, bd_ref, o_ref):
    i = pl.program_id(0)

    @pl.when(i < nu_ref[0])
    def _():
        xb = x_ref[...].astype(BF16)
        gu = jnp.dot(xb, wgu_ref[0], preferred_element_type=F32) + bgu_ref[0]
        g = jnp.minimum(gu[:, :f_dim], SWIGLU_LIMIT)
        up = jnp.clip(gu[:, f_dim:], -SWIGLU_LIMIT, SWIGLU_LIMIT)
        act = g * jax.nn.sigmoid(SWIGLU_ALPHA * g) * (up + 1.0)
        o_ref[...] = jnp.dot(act.astype(BF16), wd_ref[0], preferred_element_type=F32) + bd_ref[0]

    @pl.when(i >= nu_ref[0])
    def _():
        o_ref[...] = jnp.zeros_like(o_ref)


def _expert_call(x_grouped, block_e, n_used, w_gu, b_gu, w_d, b_d):
    n_rows, D = x_grouped.shape
    E, _, F2 = w_gu.shape
    F = F2 // 2
    n_blocks = n_rows // EXPERT_ROWS
    grid_spec = pltpu.PrefetchScalarGridSpec(
        num_scalar_prefetch=2,
        grid=(n_blocks,),
        in_specs=[
            pl.BlockSpec((EXPERT_ROWS, D), lambda i, be, nu: (jnp.minimum(i, nu[0] - 1), 0)),
            pl.BlockSpec((1, D, F2), lambda i, be, nu: (be[i], 0, 0)),
            pl.BlockSpec((1, 1, F2), lambda i, be, nu: (be[i], 0, 0)),
            pl.BlockSpec((1, F, D), lambda i, be, nu: (be[i], 0, 0)),
            pl.BlockSpec((1, 1, D), lambda i, be, nu: (be[i], 0, 0)),
        ],
        out_specs=pl.BlockSpec((EXPERT_ROWS, D), lambda i, be, nu: (i, 0)),
    )
    return pl.pallas_call(
        functools.partial(_expert_kernel, F),
        grid_spec=grid_spec,
        out_shape=jax.ShapeDtypeStruct((n_rows, D), F32),
        compiler_params=_cparams(("arbitrary",), 56),
        name="experts",
    )(block_e, n_used, x_grouped, w_gu, b_gu.reshape(E, 1, F2), w_d, b_d.reshape(E, 1, D))


def _mixer_weights(D, w_mix_in, w_uq, w_ukv, fox_f_bias):
    Wb = D // 4
    H = Wb // HEAD_DIM
    q_lora = 3 * D // 16
    kv_lora = D // 8
    half = ROPE_DIM // 2
    o_cq = 3 * Wb
    o_ckv = o_cq + q_lora
    o_kr = o_ckv + kv_lora
    o_fox = o_kr + ROPE_DIM
    o_f = o_fox + 3 * Wb
    w_convfox = jnp.concatenate([w_mix_in[:, :o_cq], w_mix_in[:, o_fox:o_f]], axis=1).astype(BF16)
    w_cq = w_mix_in[:, o_cq:o_ckv].astype(BF16)
    w_ckv = w_mix_in[:, o_ckv:o_kr].astype(BF16)
    w_kr = w_mix_in[:, o_kr:o_fox]
    zpad = jnp.zeros((D, LANES - ROPE_DIM), F32)
    w_kra = jnp.concatenate([w_kr, zpad], axis=1).astype(BF16)
    w_krb = jnp.concatenate([-w_kr[:, half:], w_kr[:, :half], zpad], axis=1).astype(BF16)
    w_f = jnp.pad(w_mix_in[:, o_f:o_f + H], ((0, 0), (0, LANES - H))).astype(BF16)
    b_f = jnp.pad(fox_f_bias, (0, LANES - H)).reshape(1, LANES)

    uq = w_uq.reshape(q_lora, H, HEAD_DIM + ROPE_DIM)
    zq = jnp.zeros((q_lora, H, QK_PAD - HEAD_DIM - ROPE_DIM), F32)
    w_qa = jnp.concatenate([uq, zq], axis=2).reshape(q_lora, H * QK_PAD).astype(BF16)
    r1 = uq[:, :, HEAD_DIM:HEAD_DIM + half]
    r2 = uq[:, :, HEAD_DIM + half:]
    w_qb = jnp.concatenate([-r2, r1, zq], axis=2).reshape(q_lora, H * LANES).astype(BF16)

    ukv = w_ukv.reshape(kv_lora, H, 2 * HEAD_DIM)
    w_k = ukv[:, :, :HEAD_DIM].reshape(kv_lora, H * HEAD_DIM).astype(BF16)
    w_v = ukv[:, :, HEAD_DIM:].reshape(kv_lora, H * HEAD_DIM).astype(BF16)
    return dict(w_convfox=w_convfox, w_cq=w_cq, w_ckv=w_ckv, w_kra=w_kra, w_krb=w_krb,
                w_f=w_f, b_f=b_f, w_qa=w_qa, w_qb=w_qb, w_k=w_k, w_v=w_v)


def _mixer(h, B, S, D, rope_cos, rope_sin, mw, conv_w, q_norm_g, kv_norm_g,
           w_branch, w_merge_gate, b_merge_gate, w_o):
    Wb = D // 4
    H = Wb // HEAD_DIM
    u = _mm(h, mw["w_convfox"], BF16, tn_pref=512, name="mix_in")
    o_a = _conv_call(u, conv_w, B, S, Wb)

    q_mla = _mla_q_call(h, mw["w_cq"], q_norm_g, mw["w_qa"], mw["w_qb"], rope_cos, rope_sin, H)
    k_mla, v_mla = _mla_kv_call(h, mw["w_ckv"], kv_norm_g, mw["w_k"], mw["w_v"],
                                mw["w_kra"], mw["w_krb"], rope_cos, rope_sin, H)
    o_b = _flash_call(q_mla, k_mla, v_mla, B, S, H, (HEAD_DIM + ROPE_DIM) ** -0.5, name="flash_mla")

    fox_scale = HEAD_DIM ** -0.5
    q_fox, k_fox = _fox_prep_call(h, u, mw["w_f"], mw["b_f"], B, S, H, 3, 4, fox_scale)
    o_c = _flash_call(q_fox, k_fox, u, B, S, H, fox_scale, v_blk0=5 * H, name="flash_fox")

    merged = _merge_call(h, w_merge_gate.astype(BF16), b_merge_gate, (o_a, o_b, o_c),
                         w_branch.astype(BF16))
    return _mm(merged, w_o.astype(BF16), F32, name="mix_out")


def _moe(x, g_pre, scale, shift, g_post, gate, w_router, b_router, w_gate_up, b_gate_up,
         w_down, b_down):
    B, S, D = x.shape
    N = B * S
    E = w_router.shape[1]
    h, top_e, top_p = _norm_router_call(x, g_pre, scale, shift, w_router, b_router)
    rank, counts8 = _rank_call(top_e)

    counts = counts8[0, :E]
    padded = (counts + EXPERT_ROWS - 1) // EXPERT_ROWS * EXPERT_ROWS
    pad_end = jnp.cumsum(padded)
    pad_start = pad_end - padded
    e_flat = top_e[:, :TOP_K].reshape(-1)
    dest = pad_start[e_flat] + rank[:, :TOP_K].reshape(-1)
    n_blocks = (N * TOP_K + E * (EXPERT_ROWS - 1) + EXPERT_ROWS - 1) // EXPERT_ROWS
    n_rows = n_blocks * EXPERT_ROWS
    block_start = jnp.arange(n_blocks, dtype=jnp.int32) * EXPERT_ROWS
    block_e = jnp.minimum(
        jnp.sum((pad_end[None, :] <= block_start[:, None]).astype(jnp.int32), axis=1), E - 1)
    n_used = (pad_end[-1:] // EXPERT_ROWS).astype(jnp.int32)

    x_grouped = _dispatch_call(h.reshape(N, D), dest.astype(jnp.int32),
                               (pad_start + counts).astype(jnp.int32),
                               pad_end.astype(jnp.int32), n_used, n_rows)
    y_grouped = _expert_call(x_grouped, block_e, n_used, w_gate_up.astype(BF16), b_gate_up,
                             w_down.astype(BF16), b_down)
    return _combine_post_call(x, y_grouped, dest.astype(jnp.int32), top_p, g_post, gate)


def kernel(x, c, positions, w_ada, b_ada, g_mix_pre, g_mix_post, g_ffn_pre, g_ffn_post, w_mix_in, conv_w, q_norm_g, kv_norm_g, w_uq, w_ukv, fox_f_bias, w_branch, w_merge_gate, b_merge_gate, w_o, w_router, b_router, w_gate_up, b_gate_up, w_down, b_down):
    B, S, D = x.shape
    L = w_ada.shape[0]
    N = B * S

    c8 = jnp.pad(c, ((0, 8 - B), (0, 0)))
    ada = _ada(c8, w_ada, b_ada)[:, :B].reshape(L, B, N_ADA, 1, D)

    half = ROPE_DIM // 2
    inv_freq = ROPE_THETA ** (-jnp.arange(half, dtype=F32) / half)
    ang = positions.astype(F32).reshape(N, 1) * inv_freq
    zer = jnp.zeros((N, LANES - ROPE_DIM), F32)
    rope_cos = jnp.concatenate([jnp.cos(ang), jnp.cos(ang), zer], axis=1)
    rope_sin = jnp.concatenate([jnp.sin(ang), jnp.sin(ang), zer], axis=1)

    for l in range(L):
        shift_m, scale_m, gate_m = ada[l, :, 0], ada[l, :, 1], ada[l, :, 2]
        shift_f, scale_f, gate_f = ada[l, :, 3], ada[l, :, 4], ada[l, :, 5]

        h = _norm_mod_call(x, g_mix_pre[l], scale_m, shift_m).reshape(N, D)
        mw = _mixer_weights(D, w_mix_in[l], w_uq[l], w_ukv[l], fox_f_bias[l])
        y = _mixer(h, B, S, D, rope_cos, rope_sin, mw, conv_w[l], q_norm_g[l], kv_norm_g[l],
                   w_branch[l], w_merge_gate[l], b_merge_gate[l], w_o[l])
        x = _post_call(x, y, g_mix_post[l], gate_m)

        x = _moe(x, g_ffn_pre[l], scale_f, shift_f, g_ffn_post[l], gate_f, w_router[l],
                 b_router[l], w_gate_up[l], b_gate_up[l], w_down[l], b_down[l])
    return x
```

```python
import functools

import jax
import jax.numpy as jnp
from jax import lax
from jax.experimental import pallas as pl
from jax.experimental.pallas import tpu as pltpu

NORM_EPS = 1e-6
ROPE_THETA = 10000.0
HEAD_DIM = 128
ROPE_DIM = 64
CONV_K = 3
TOP_K = 4
SWIGLU_ALPHA = 1.702
SWIGLU_LIMIT = 7.0
N_ADA = 6
LANES = 128
QK_PAD = 2 * LANES
EXPERT_ROWS = 256
MIB = 1024 * 1024

F32 = jnp.float32
BF16 = jnp.bfloat16


def _cparams(sem, vmem_mib=48):
    return pltpu.CompilerParams(dimension_semantics=sem, vmem_limit_bytes=vmem_mib * MIB)


def _tile(n, pref):
    t = min(pref, n)
    while n % t:
        t //= 2
    return t


def _pack_bf16_pairs(x):
    n = x.shape[1] // 2
    lo = lax.bitcast_convert_type(x[:, :n].astype(BF16).astype(F32), jnp.uint32)
    hi = lax.bitcast_convert_type(x[:, n:].astype(BF16).astype(F32), jnp.uint32)
    return hi | (lo >> 16)


def _unpack_bf16_pairs(w):
    lo = lax.bitcast_convert_type(w << 16, F32)
    hi = lax.bitcast_convert_type(w & jnp.uint32(0xFFFF0000), F32)
    return jnp.concatenate([lo, hi], axis=1)


def _split3(x):
    hi = x.astype(BF16)
    r1 = x - hi.astype(F32)
    mid = r1.astype(BF16)
    lo = (r1 - mid.astype(F32)).astype(BF16)
    return hi, mid, lo


def _ada_kernel(c_ref, w_ref, b_ref, o_ref):
    c = c_ref[...]
    c_act = (c * jax.nn.sigmoid(c)).astype(BF16)
    acc = jnp.dot(c_act, w_ref[0].astype(BF16), preferred_element_type=F32)
    o_ref[0] = acc + b_ref[0]


def _ada(c8, w_ada, b_ada):
    L, D, N = w_ada.shape
    tn = _tile(N, 1024)
    return pl.pallas_call(
        _ada_kernel,
        grid=(L, N // tn),
        in_specs=[
            pl.BlockSpec((8, D), lambda l, j: (0, 0)),
            pl.BlockSpec((1, D, tn), lambda l, j: (l, 0, j)),
            pl.BlockSpec((1, 1, tn), lambda l, j: (l, 0, j)),
        ],
        out_specs=pl.BlockSpec((1, 8, tn), lambda l, j: (l, 0, j)),
        out_shape=jax.ShapeDtypeStruct((L, 8, N), F32),
        compiler_params=_cparams(("parallel", "parallel"), 56),
        name="ada",
    )(c8, w_ada, b_ada.reshape(L, 1, N))


def _norm_mod(x, g, scale, shift):
    xf = x
    y = xf * lax.rsqrt(jnp.mean(xf * xf, axis=-1, keepdims=True) + NORM_EPS)
    return (y * g) * (1.0 + scale) + shift


def _norm_mod_kernel(x_ref, g_ref, sc_ref, sh_ref, o_ref):
    o_ref[0] = _norm_mod(x_ref[0], g_ref[...], sc_ref[0], sh_ref[0]).astype(o_ref.dtype)


def _norm_mod_call(x, g, scale, shift):
    B, S, D = x.shape
    ts = _tile(S, 256)
    return pl.pallas_call(
        _norm_mod_kernel,
        grid=(B, S // ts),
        in_specs=[
            pl.BlockSpec((1, ts, D), lambda b, i: (b, i, 0)),
            pl.BlockSpec((1, D), lambda b, i: (0, 0)),
            pl.BlockSpec((1, 1, D), lambda b, i: (b, 0, 0)),
            pl.BlockSpec((1, 1, D), lambda b, i: (b, 0, 0)),
        ],
        out_specs=pl.BlockSpec((1, ts, D), lambda b, i: (b, i, 0)),
        out_shape=jax.ShapeDtypeStruct((B, S, D), BF16),
        compiler_params=_cparams(("parallel", "parallel")),
        name="norm_mod",
    )(x, g.reshape(1, D), scale, shift)


def _norm_router_kernel(n_experts, x_ref, g_ref, sc_ref, sh_ref, wr_hi_ref, wr_lo_ref, br_ref,
                        h_ref, e_ref, p_ref):
    h = _norm_mod(x_ref[0], g_ref[...], sc_ref[0], sh_ref[0])
    h_ref[0] = _pack_bf16_pairs(h)
    h_hi = h.astype(BF16)
    h_lo = (h - h_hi.astype(F32)).astype(BF16)
    logits = (jnp.dot(h_hi, wr_hi_ref[...], preferred_element_type=F32)
              + jnp.dot(h_hi, wr_lo_ref[...], preferred_element_type=F32)
              + jnp.dot(h_lo, wr_hi_ref[...], preferred_element_type=F32)) + br_ref[...]
    lane = lax.broadcasted_iota(jnp.int32, logits.shape, 1)
    vals = jnp.where(lane < n_experts, logits, -jnp.inf)
    tops, idxs = [], []
    for _ in range(TOP_K):
        m = jnp.max(vals, axis=-1, keepdims=True)
        idx = jnp.min(jnp.where(vals == m, lane, LANES), axis=-1, keepdims=True)
        tops.append(m)
        idxs.append(idx)
        vals = jnp.where(lane == idx, -jnp.inf, vals)
    exps = [jnp.exp(t - tops[0]) for t in tops]
    denom = exps[0]
    for e in exps[1:]:
        denom = denom + e
    e_out = jnp.zeros(logits.shape, jnp.int32)
    p_out = jnp.zeros(logits.shape, F32)
    for k in range(TOP_K):
        e_out = jnp.where(lane == k, idxs[k], e_out)
        p_out = jnp.where(lane == k, exps[k] / denom, p_out)
    e_ref[...] = e_out
    p_ref[...] = p_out


def _norm_router_call(x, g, scale, shift, w_router, b_router):
    B, S, D = x.shape
    E = w_router.shape[1]
    ts = _tile(S, 256)
    nb = S // ts
    wr = jnp.pad(w_router, ((0, 0), (0, LANES - E)))
    wr_hi = wr.astype(BF16)
    wr_lo = (wr - wr_hi.astype(F32)).astype(BF16)
    br = jnp.pad(b_router, (0, LANES - E)).reshape(1, LANES)
    return pl.pallas_call(
        functools.partial(_norm_router_kernel, E),
        grid=(B, nb),
        in_specs=[
            pl.BlockSpec((1, ts, D), lambda b, i: (b, i, 0)),
            pl.BlockSpec((1, D), lambda b, i: (0, 0)),
            pl.BlockSpec((1, 1, D), lambda b, i: (b, 0, 0)),
            pl.BlockSpec((1, 1, D), lambda b, i: (b, 0, 0)),
            pl.BlockSpec((D, LANES), lambda b, i: (0, 0)),
            pl.BlockSpec((D, LANES), lambda b, i: (0, 0)),
            pl.BlockSpec((1, LANES), lambda b, i: (0, 0)),
        ],
        out_specs=[
            pl.BlockSpec((1, ts, D // 2), lambda b, i: (b, i, 0)),
            pl.BlockSpec((ts, LANES), lambda b, i: (b * nb + i, 0)),
            pl.BlockSpec((ts, LANES), lambda b, i: (b * nb + i, 0)),
        ],
        out_shape=[
            jax.ShapeDtypeStruct((B, S, D // 2), jnp.uint32),
            jax.ShapeDtypeStruct((B * S, LANES), jnp.int32),
            jax.ShapeDtypeStruct((B * S, LANES), F32),
        ],
        compiler_params=_cparams(("parallel", "parallel")),
        name="norm_router",
    )(x, g.reshape(1, D), scale, shift, wr_hi, wr_lo, br)


def _mm_kernel(a_ref, w_ref, o_ref):
    o_ref[...] = jnp.dot(a_ref[...], w_ref[...], preferred_element_type=F32).astype(o_ref.dtype)


def _mm(a, w, out_dtype, tm_pref=1024, tn_pref=512, name="mm"):
    M, K = a.shape
    N = w.shape[1]
    tm = _tile(M, tm_pref)
    tn = _tile(N, tn_pref)
    return pl.pallas_call(
        _mm_kernel,
        grid=(M // tm, N // tn),
        in_specs=[
            pl.BlockSpec((tm, K), lambda i, j: (i, 0)),
            pl.BlockSpec((K, tn), lambda i, j: (0, j)),
        ],
        out_specs=pl.BlockSpec((tm, tn), lambda i, j: (i, j)),
        out_shape=jax.ShapeDtypeStruct((M, N), out_dtype),
        compiler_params=_cparams(("parallel", "parallel"), 56),
        name=name,
    )(a, w)


def _conv_kernel(b_ref, c_ref, h_ref, w_ref, o_ref):
    g = c_ref[...].astype(F32) * h_ref[...].astype(F32)
    row = lax.broadcasted_iota(jnp.int32, g.shape, 0)
    g1 = jnp.where(row >= 1, pltpu.roll(g, 1, 0), 0.0)
    g2 = jnp.where(row >= 2, pltpu.roll(g, 2, 0), 0.0)
    w = w_ref[...]
    conv = w[0:1] * g2 + w[1:2] * g1 + w[2:3] * g
    o_ref[...] = (b_ref[...].astype(F32) * conv).astype(o_ref.dtype)


def _conv_call(u, conv_w, B, S, C):
    tc = _tile(C, 256)
    nc = C // tc
    w8 = jnp.pad(conv_w, ((0, 8 - CONV_K), (0, 0)))
    return pl.pallas_call(
        _conv_kernel,
        grid=(B, nc),
        in_specs=[
            pl.BlockSpec((S, tc), lambda b, j: (b, j)),
            pl.BlockSpec((S, tc), lambda b, j: (b, nc + j)),
            pl.BlockSpec((S, tc), lambda b, j: (b, 2 * nc + j)),
            pl.BlockSpec((8, tc), lambda b, j: (0, j)),
        ],
        out_specs=pl.BlockSpec((S, tc), lambda b, j: (b, j)),
        out_shape=jax.ShapeDtypeStruct((B * S, C), BF16),
        compiler_params=_cparams(("parallel", "parallel")),
        name="gated_conv",
    )(u, u, u, w8)


def _rms(x, g):
    return x * lax.rsqrt(jnp.mean(x * x, axis=-1, keepdims=True) + NORM_EPS) * g


def _mla_q_kernel(n_heads, h_ref, wc_ref, g_ref, wa_ref, wb_ref, cos_ref, sin_ref, q_ref):
    c_q = jnp.dot(h_ref[...], wc_ref[...], preferred_element_type=F32)
    n = _rms(c_q, g_ref[...]).astype(BF16)
    a = jnp.dot(n, wa_ref[...], preferred_element_type=F32)
    bm = jnp.dot(n, wb_ref[...], preferred_element_type=F32)
    cos = cos_ref[...]
    sin = sin_ref[...]
    for hd in range(n_heads):
        o = hd * QK_PAD
        q_ref[:, o:o + LANES] = a[:, o:o + LANES].astype(q_ref.dtype)
        rot = a[:, o + LANES:o + QK_PAD] * cos + bm[:, hd * LANES:(hd + 1) * LANES] * sin
        q_ref[:, o + LANES:o + QK_PAD] = rot.astype(q_ref.dtype)


def _mla_q_call(h, w_cq, q_norm_g, w_a, w_b, cos_t, sin_t, n_heads):
    M, D = h.shape
    R = w_cq.shape[1]
    tm = _tile(M, 512)
    const = lambda i: (0, 0)
    return pl.pallas_call(
        functools.partial(_mla_q_kernel, n_heads),
        grid=(M // tm,),
        in_specs=[
            pl.BlockSpec((tm, D), lambda i: (i, 0)),
            pl.BlockSpec((D, R), const),
            pl.BlockSpec((1, R), const),
            pl.BlockSpec((R, n_heads * QK_PAD), const),
            pl.BlockSpec((R, n_heads * LANES), const),
            pl.BlockSpec((tm, LANES), lambda i: (i, 0)),
            pl.BlockSpec((tm, LANES), lambda i: (i, 0)),
        ],
        out_specs=pl.BlockSpec((tm, n_heads * QK_PAD), lambda i: (i, 0)),
        out_shape=jax.ShapeDtypeStruct((M, n_heads * QK_PAD), BF16),
        compiler_params=_cparams(("parallel",), 56),
        name="mla_q",
    )(h, w_cq, q_norm_g.reshape(1, R), w_a, w_b, cos_t, sin_t)


def _mla_kv_kernel(n_heads, h_ref, wc_ref, g_ref, wk_ref, wv_ref, wra_ref, wrb_ref,
                   cos_ref, sin_ref, k_ref, v_ref):
    hb = h_ref[...]
    c_kv = jnp.dot(hb, wc_ref[...], preferred_element_type=F32)
    n = _rms(c_kv, g_ref[...]).astype(BF16)
    k_nope = jnp.dot(n, wk_ref[...], preferred_element_type=F32)
    v_ref[...] = jnp.dot(n, wv_ref[...], preferred_element_type=F32).astype(v_ref.dtype)
    ra = jnp.dot(hb, wra_ref[...], preferred_element_type=F32)
    rb = jnp.dot(hb, wrb_ref[...], preferred_element_type=F32)
    k_rope = (ra * cos_ref[...] + rb * sin_ref[...]).astype(k_ref.dtype)
    for hd in range(n_heads):
        o = hd * QK_PAD
        k_ref[:, o:o + LANES] = k_nope[:, hd * LANES:(hd + 1) * LANES].astype(k_ref.dtype)
        k_ref[:, o + LANES:o + QK_PAD] = k_rope


def _mla_kv_call(h, w_ckv, kv_norm_g, w_k, w_v, w_ra, w_rb, cos_t, sin_t, n_heads):
    M, D = h.shape
    R = w_ckv.shape[1]
    tm = _tile(M, 512)
    const = lambda i: (0, 0)
    return pl.pallas_call(
        functools.partial(_mla_kv_kernel, n_heads),
        grid=(M // tm,),
        in_specs=[
            pl.BlockSpec((tm, D), lambda i: (i, 0)),
            pl.BlockSpec((D, R), const),
            pl.BlockSpec((1, R), const),
            pl.BlockSpec((R, n_heads * LANES), const),
            pl.BlockSpec((R, n_heads * LANES), const),
            pl.BlockSpec((D, LANES), const),
            pl.BlockSpec((D, LANES), const),
            pl.BlockSpec((tm, LANES), lambda i: (i, 0)),
            pl.BlockSpec((tm, LANES), lambda i: (i, 0)),
        ],
        out_specs=[
            pl.BlockSpec((tm, n_heads * QK_PAD), lambda i: (i, 0)),
            pl.BlockSpec((tm, n_heads * LANES), lambda i: (i, 0)),
        ],
        out_shape=[
            jax.ShapeDtypeStruct((M, n_heads * QK_PAD), BF16),
            jax.ShapeDtypeStruct((M, n_heads * LANES), BF16),
        ],
        compiler_params=_cparams(("parallel",), 56),
        name="mla_kv",
    )(h, w_ckv, kv_norm_g.reshape(1, R), w_k, w_v, w_ra, w_rb, cos_t, sin_t)


def _fox_prep_kernel(n_heads, inv_scale, h_ref, wf_ref, bf_ref, q_ref, k_ref,
                     qo_ref, ko_ref, carry_ref):
    @pl.when(pl.program_id(1) == 0)
    def _():
        carry_ref[...] = jnp.zeros_like(carry_ref)

    f = jnp.dot(h_ref[...], wf_ref[...], preferred_element_type=F32) + bf_ref[...]
    log_f = jax.nn.log_sigmoid(f)
    ts = log_f.shape[0]
    row = lax.broadcasted_iota(jnp.int32, (ts, ts), 0)
    col = lax.broadcasted_iota(jnp.int32, (ts, ts), 1)
    tri = jnp.where(col <= row, 1.0, 0.0).astype(BF16)
    cum = carry_ref[0:1, :]
    for part in _split3(log_f):
        cum = cum + jnp.dot(tri, part, preferred_element_type=F32)
    carry_ref[...] = jnp.broadcast_to(cum[ts - 1:ts, :], carry_ref.shape)

    lane = lax.broadcasted_iota(jnp.int32, (ts, LANES), 1)
    cum_s = cum * inv_scale
    for hd in range(n_heads):
        colv = jnp.broadcast_to(cum_s[:, hd:hd + 1], (ts, LANES))
        hi, mid, lo = (part.astype(F32) for part in _split3(colv))
        q_aug = jnp.where(lane == 0, hi, jnp.where(lane == 1, mid, jnp.where(
            lane == 2, lo, jnp.where(lane < 6, 1.0, 0.0))))
        k_aug = jnp.where(lane < 3, 1.0, jnp.where(lane == 3, -hi, jnp.where(
            lane == 4, -mid, jnp.where(lane == 5, -lo, 0.0))))
        o = hd * QK_PAD
        qo_ref[:, o:o + LANES] = q_ref[:, hd * LANES:(hd + 1) * LANES]
        qo_ref[:, o + LANES:o + QK_PAD] = q_aug.astype(BF16)
        ko_ref[:, o:o + LANES] = k_ref[:, hd * LANES:(hd + 1) * LANES]
        ko_ref[:, o + LANES:o + QK_PAD] = k_aug.astype(BF16)


def _fox_prep_call(h, u, w_f, b_f, B, S, n_heads, q_blk, k_blk, scale):
    M, D = h.shape
    W = n_heads * LANES
    ts = _tile(S, 512)
    nb = S // ts
    row = lambda b, i: (b * nb + i, 0)
    return pl.pallas_call(
        functools.partial(_fox_prep_kernel, n_heads, 1.0 / scale),
        grid=(B, nb),
        in_specs=[
            pl.BlockSpec((ts, D), row),
            pl.BlockSpec((D, LANES), lambda b, i: (0, 0)),
            pl.BlockSpec((1, LANES), lambda b, i: (0, 0)),
            pl.BlockSpec((ts, W), lambda b, i: (b * nb + i, q_blk)),
            pl.BlockSpec((ts, W), lambda b, i: (b * nb + i, k_blk)),
        ],
        out_specs=[
            pl.BlockSpec((ts, n_heads * QK_PAD), row),
            pl.BlockSpec((ts, n_heads * QK_PAD), row),
        ],
        out_shape=[
            jax.ShapeDtypeStruct((M, n_heads * QK_PAD), BF16),
            jax.ShapeDtypeStruct((M, n_heads * QK_PAD), BF16),
        ],
        scratch_shapes=[pltpu.VMEM((8, LANES), F32)],
        compiler_params=_cparams(("parallel", "arbitrary")),
        name="fox_prep",
    )(h, w_f, b_f, u, u)


def _flash_kernel(scale_log2, t, hp, q_ref, k_ref, v_ref, o_ref, m_ref, l_ref, acc_ref):
    i = pl.program_id(2)
    m_ref[...] = jnp.full_like(m_ref, -jnp.inf)
    l_ref[...] = jnp.zeros_like(l_ref)
    acc_ref[...] = jnp.zeros_like(acc_ref)

    def tile(j, masked):
        off = pl.multiple_of(j * t, t)
        for hh in range(hp):
            q = q_ref[:, hh * QK_PAD:(hh + 1) * QK_PAD]
            k = k_ref[pl.ds(off, t), hh * QK_PAD:(hh + 1) * QK_PAD]
            v = v_ref[pl.ds(off, t), hh * HEAD_DIM:(hh + 1) * HEAD_DIM]
            s = lax.dot_general(q, k, (((1,), (1,)), ((), ())),
                                preferred_element_type=F32) * scale_log2
            if masked:
                row = lax.broadcasted_iota(jnp.int32, s.shape, 0)
                col = lax.broadcasted_iota(jnp.int32, s.shape, 1)
                s = jnp.where(col <= row, s, -jnp.inf)
            parts = [s[:, c * LANES:(c + 1) * LANES] for c in range(t // LANES)]
            part_max = parts[0]
            for part in parts[1:]:
                part_max = jnp.maximum(part_max, part)
            m_prev = m_ref[hh]
            m_new = jnp.maximum(m_prev, jnp.max(part_max, axis=-1, keepdims=True))
            alpha = jnp.exp2(m_prev - m_new)
            probs = [jnp.exp2(part - m_new) for part in parts]
            l_part = probs[0]
            for pr in probs[1:]:
                l_part = l_part + pr
            l_ref[hh] = alpha * l_ref[hh] + l_part
            p = jnp.concatenate(probs, axis=1).astype(v.dtype)
            acc_ref[hh] = alpha * acc_ref[hh] + jnp.dot(p, v, preferred_element_type=F32)
            m_ref[hh] = m_new

    def body(j, c):
        tile(j, False)
        return c

    lax.fori_loop(0, i, body, 0)
    tile(i, True)
    for hh in range(hp):
        l_row = jnp.sum(l_ref[hh], axis=-1, keepdims=True)
        o_ref[:, hh * HEAD_DIM:(hh + 1) * HEAD_DIM] = (acc_ref[hh] / l_row).astype(o_ref.dtype)


def _flash_call(q, k, v, B, S, n_heads, scale, v_blk0=0, name="flash"):
    M = q.shape[0]
    t = _tile(S, 1024)
    nq = S // t
    hp = 2
    assert n_heads % hp == 0 and v_blk0 % hp == 0
    return pl.pallas_call(
        functools.partial(_flash_kernel, scale * 1.4426950408889634, t, hp),
        grid=(B, n_heads // hp, nq),
        in_specs=[
            pl.BlockSpec((t, hp * QK_PAD), lambda b, h, i: (b * nq + i, h)),
            pl.BlockSpec((S, hp * QK_PAD), lambda b, h, i: (b, h)),
            pl.BlockSpec((S, hp * HEAD_DIM), lambda b, h, i: (b, v_blk0 // hp + h)),
        ],
        out_specs=pl.BlockSpec((t, hp * HEAD_DIM), lambda b, h, i: (b * nq + i, h)),
        out_shape=jax.ShapeDtypeStruct((M, n_heads * HEAD_DIM), BF16),
        scratch_shapes=[
            pltpu.VMEM((hp, t, LANES), F32),
            pltpu.VMEM((hp, t, LANES), F32),
            pltpu.VMEM((hp, t, HEAD_DIM), F32),
        ],
        compiler_params=_cparams(("parallel", "parallel", "arbitrary")),
        name=name,
    )(q, k, v)


def _merge_kernel(n_br, h_ref, wg_ref, bg_ref, *rest):
    br_refs = rest[:n_br]
    wb_ref = rest[n_br]
    o_ref = rest[n_br + 1]
    hb = h_ref[...]
    merged = None
    for n in range(n_br):
        gate = jax.nn.sigmoid(jnp.dot(hb, wg_ref[n], preferred_element_type=F32) + bg_ref[n])
        proj = jnp.dot(br_refs[n][...], wb_ref[n], preferred_element_type=F32)
        merged = gate * proj if merged is None else merged + gate * proj
    o_ref[...] = merged.astype(o_ref.dtype)


def _merge_call(h, w_gate, b_gate, branches, w_branch):
    M, D = h.shape
    n_br, W, _ = w_branch.shape
    tm = _tile(M, 512)
    tn = _tile(D, 256)
    in_specs = [
        pl.BlockSpec((tm, D), lambda i, j: (i, 0)),
        pl.BlockSpec((n_br, D, tn), lambda i, j: (0, 0, j)),
        pl.BlockSpec((n_br, 1, tn), lambda i, j: (0, 0, j)),
    ]
    in_specs += [pl.BlockSpec((tm, W), lambda i, j: (i, 0)) for _ in range(n_br)]
    in_specs += [pl.BlockSpec((n_br, W, tn), lambda i, j: (0, 0, j))]
    return pl.pallas_call(
        functools.partial(_merge_kernel, n_br),
        grid=(M // tm, D // tn),
        in_specs=in_specs,
        out_specs=pl.BlockSpec((tm, tn), lambda i, j: (i, j)),
        out_shape=jax.ShapeDtypeStruct((M, D), BF16),
        compiler_params=_cparams(("parallel", "parallel"), 56),
        name="gated_merge",
    )(h, w_gate, b_gate.reshape(n_br, 1, D), *branches, w_branch)


def _post_kernel(x_ref, y_ref, g_ref, gate_ref, o_ref):
    o_ref[0] = x_ref[0] + gate_ref[0] * _rms(y_ref[0].astype(F32), g_ref[...])


def _post_call(x, y, g, gate):
    B, S, D = x.shape
    ts = _tile(S, 256)
    blk = pl.BlockSpec((1, ts, D), lambda b, i: (b, i, 0))
    return pl.pallas_call(
        _post_kernel,
        grid=(B, S // ts),
        in_specs=[blk, blk, pl.BlockSpec((1, D), lambda b, i: (0, 0)),
                  pl.BlockSpec((1, 1, D), lambda b, i: (b, 0, 0))],
        out_specs=blk,
        out_shape=jax.ShapeDtypeStruct((B, S, D), F32),
        compiler_params=_cparams(("parallel", "parallel")),
        name="post_residual",
    )(x, y.reshape(B, S, D), g.reshape(1, D), gate)


def _combine_post_kernel(ts, n_steps, dest_ref, x_ref, y_hbm, p_ref, g_ref, gate_ref, o_ref,
                         ybuf, sems):
    s = pl.program_id(0)

    def issue(step, slot):
        def tok_body(r, c):
            t = step * ts + r
            for k in range(TOP_K):
                pltpu.make_async_copy(y_hbm.at[pl.ds(dest_ref[t * TOP_K + k], 1)],
                                      ybuf.at[slot, k, pl.ds(r, 1)], sems.at[slot]).start()
            return c
        lax.fori_loop(0, ts, tok_body, 0)

    @pl.when(s == 0)
    def _():
        issue(0, 0)

    @pl.when(s + 1 < n_steps)
    def _():
        issue(s + 1, (s + 1) % 2)

    slot = s % 2
    pltpu.make_async_copy(ybuf.at[slot], ybuf.at[slot], sems.at[slot]).wait()
    p = p_ref[...]
    y = _unpack_bf16_pairs(ybuf[slot, 0]) * p[:, 0:1]
    for k in range(1, TOP_K):
        y = y + _unpack_bf16_pairs(ybuf[slot, k]) * p[:, k:k + 1]
    o_ref[0] = x_ref[0] + gate_ref[0] * _rms(y, g_ref[...])


def _combine_post_call(x, y_grouped, dest_flat, top_p, g, gate):
    B, S, D = x.shape
    ts = _tile(S, 128)
    nb = S // ts
    blk = pl.BlockSpec((1, ts, D), lambda s, d: (s // nb, s % nb, 0))
    grid_spec = pltpu.PrefetchScalarGridSpec(
        num_scalar_prefetch=1,
        grid=(B * nb,),
        in_specs=[blk,
                  pl.BlockSpec(memory_space=pl.ANY),
                  pl.BlockSpec((ts, LANES), lambda s, d: (s, 0)),
                  pl.BlockSpec((1, D), lambda s, d: (0, 0)),
                  pl.BlockSpec((1, 1, D), lambda s, d: (s // nb, 0, 0))],
        out_specs=blk,
        scratch_shapes=[pltpu.VMEM((2, TOP_K, ts, D // 2), jnp.uint32),
                        pltpu.SemaphoreType.DMA((2,))],
    )
    return pl.pallas_call(
        functools.partial(_combine_post_kernel, ts, B * nb),
        grid_spec=grid_spec,
        out_shape=jax.ShapeDtypeStruct((B, S, D), F32),
        compiler_params=_cparams(("arbitrary",)),
        name="combine_post",
    )(dest_flat, x, y_grouped, top_p, g.reshape(1, D), gate)


def _rank_kernel(e_ref, rank_ref, cnt_ref, carry_ref):
    @pl.when(pl.program_id(0) == 0)
    def _():
        carry_ref[...] = jnp.zeros_like(carry_ref)

    e = e_ref[...]
    tb = e.shape[0]
    lane = lax.broadcasted_iota(jnp.int32, e.shape, 1)
    sel = [jnp.broadcast_to(e[:, k:k + 1], e.shape) == lane for k in range(TOP_K)]
    onehot = jnp.zeros(e.shape, F32)
    for s in sel:
        onehot = onehot + jnp.where(s, 1.0, 0.0)
    row = lax.broadcasted_iota(jnp.int32, (tb, tb), 0)
    col = lax.broadcasted_iota(jnp.int32, (tb, tb), 1)
    strict = jnp.where(col < row, 1.0, 0.0).astype(BF16)
    before = carry_ref[0:1, :] + jnp.dot(strict, onehot.astype(BF16), preferred_element_type=F32)
    rank = jnp.zeros(e.shape, F32)
    for k in range(TOP_K):
        r_k = jnp.sum(jnp.where(sel[k], before, 0.0), axis=-1, keepdims=True)
        rank = jnp.where(lane == k, r_k, rank)
    rank_ref[...] = rank.astype(jnp.int32)
    total = carry_ref[0:1, :] + jnp.sum(onehot, axis=0, keepdims=True)
    carry_ref[...] = jnp.broadcast_to(total, carry_ref.shape)
    cnt_ref[...] = jnp.broadcast_to(total, cnt_ref.shape).astype(jnp.int32)


def _rank_call(top_e):
    N = top_e.shape[0]
    tb = _tile(N, 512)
    return pl.pallas_call(
        _rank_kernel,
        grid=(N // tb,),
        in_specs=[pl.BlockSpec((tb, LANES), lambda i: (i, 0))],
        out_specs=[pl.BlockSpec((tb, LANES), lambda i: (i, 0)),
                   pl.BlockSpec((8, LANES), lambda i: (0, 0))],
        out_shape=[jax.ShapeDtypeStruct((N, LANES), jnp.int32),
                   jax.ShapeDtypeStruct((8, LANES), jnp.int32)],
        scratch_shapes=[pltpu.VMEM((8, LANES), F32)],
        compiler_params=_cparams(("arbitrary",)),
        name="route_rank",
    )(top_e)


def _row_copy(src, dst, s_row, d_row, sem):
    return pltpu.make_async_copy(src.at[pl.ds(s_row, 1)], dst.at[pl.ds(d_row, 1)], sem)


def _rows_wait(ref, n, sem):
    pltpu.make_async_copy(ref.at[pl.ds(0, n)], ref.at[pl.ds(0, n)], sem).wait()


def _dispatch_kernel(tb, n_experts, n_blocks, dest_ref, zlo_ref, zhi_ref, nblk_ref,
                     h_ref, xg_ref, zero_ref, sem):
    i = pl.program_id(0)

    @pl.when(i == 0)
    def _():
        zero_ref[...] = jnp.zeros_like(zero_ref)

        def exp_body(e, carry):
            def issue(r, c2):
                _row_copy(zero_ref, xg_ref, 0, r, sem).start()
                return c2

            def wait(r, c2):
                _row_copy(zero_ref, xg_ref, 0, r, sem).wait()
                return c2

            lax.fori_loop(zlo_ref[e], zhi_ref[e], issue, 0)
            lax.fori_loop(zlo_ref[e], zhi_ref[e], wait, 0)
            return carry

        lax.fori_loop(0, n_experts, exp_body, 0)

        def tail_body(b, carry):
            cp = pltpu.make_async_copy(
                zero_ref,
                xg_ref.at[pl.ds(pl.multiple_of(b * EXPERT_ROWS, EXPERT_ROWS), EXPERT_ROWS)], sem)
            cp.start()
            cp.wait()
            return carry

        lax.fori_loop(nblk_ref[0], n_blocks, tail_body, 0)

    def tok_body(r, carry):
        t = i * tb + r
        for k in range(TOP_K):
            _row_copy(h_ref, xg_ref, r, dest_ref[t * TOP_K + k], sem).start()
        return carry

    lax.fori_loop(0, tb, tok_body, 0)
    _rows_wait(xg_ref, tb * TOP_K, sem)


def _dispatch_call(h, dest_flat, zlo, zhi, n_used, n_rows):
    N, W = h.shape
    E = zlo.shape[0]
    tb = _tile(N, 256)
    grid_spec = pltpu.PrefetchScalarGridSpec(
        num_scalar_prefetch=4,
        grid=(N // tb,),
        in_specs=[pl.BlockSpec((tb, W), lambda i, *_: (i, 0))],
        out_specs=pl.BlockSpec(memory_space=pl.ANY),
        scratch_shapes=[pltpu.VMEM((EXPERT_ROWS, W), h.dtype), pltpu.SemaphoreType.DMA(())],
    )
    return pl.pallas_call(
        functools.partial(_dispatch_kernel, tb, E, n_rows // EXPERT_ROWS),
        grid_spec=grid_spec,
        out_shape=jax.ShapeDtypeStruct((n_rows, W), h.dtype),
        compiler_params=pltpu.CompilerParams(dimension_semantics=("arbitrary",),
                                             has_side_effects=True),
        name="moe_dispatch",
    )(dest_flat, zlo, zhi, n_used, h)


def _expert_kernel(f_dim, be_ref, nu_ref, x_ref, wgu_ref, bgu_ref, wd_ref, bd_ref, o_ref):
    i = pl.program_id(0)

    @pl.when(i < nu_ref[0])
    def _():
        xb = _unpack_bf16_pairs(x_ref[...]).astype(BF16)
        gu = jnp.dot(xb, wgu_ref[0], preferred_element_type=F32) + bgu_ref[0]
        g = jnp.minimum(gu[:, :f_dim], SWIGLU_LIMIT)
        up = jnp.clip(gu[:, f_dim:], -SWIGLU_LIMIT, SWIGLU_LIMIT)
        act = g * jax.nn.sigmoid(SWIGLU_ALPHA * g) * (up + 1.0)
        y = jnp.dot(act.astype(BF16), wd_ref[0], preferred_element_type=F32) + bd_ref[0]
        o_ref[...] = _pack_bf16_pairs(y)

    @pl.when(i >= nu_ref[0])
    def _():
        o_ref[...] = jnp.zeros_like(o_ref)


def _expert_call(x_grouped, block_e, n_used, w_gu, b_gu, w_d, b_d):
    E, D, F2 = w_gu.shape
    n_rows = x_grouped.shape[0]
    F = F2 // 2
    n_blocks = n_rows // EXPERT_ROWS
    grid_spec = pltpu.PrefetchScalarGridSpec(
        num_scalar_prefetch=2,
        grid=(n_blocks,),
        in_specs=[
            pl.BlockSpec((EXPERT_ROWS, D // 2), lambda i, be, nu: (jnp.minimum(i, nu[0] - 1), 0)),
            pl.BlockSpec((1, D, F2), lambda i, be, nu: (be[i], 0, 0)),
            pl.BlockSpec((1, 1, F2), lambda i, be, nu: (be[i], 0, 0)),
            pl.BlockSpec((1, F, D), lambda i, be, nu: (be[i], 0, 0)),
            pl.BlockSpec((1, 1, D), lambda i, be, nu: (be[i], 0, 0)),
        ],
        out_specs=pl.BlockSpec((EXPERT_ROWS, D // 2), lambda i, be, nu: (i, 0)),
    )
    return pl.pallas_call(
        functools.partial(_expert_kernel, F),
        grid_spec=grid_spec,
        out_shape=jax.ShapeDtypeStruct((n_rows, D // 2), jnp.uint32),
        compiler_params=_cparams(("arbitrary",), 56),
        name="experts",
    )(block_e, n_used, x_grouped, w_gu, b_gu.reshape(E, 1, F2), w_d, b_d.reshape(E, 1, D))


def _mixer_weights(D, w_mix_in, w_uq, w_ukv, fox_f_bias):
    Wb = D // 4
    H = Wb // HEAD_DIM
    q_lora = 3 * D // 16
    kv_lora = D // 8
    half = ROPE_DIM // 2
    o_cq = 3 * Wb
    o_ckv = o_cq + q_lora
    o_kr = o_ckv + kv_lora
    o_fox = o_kr + ROPE_DIM
    o_f = o_fox + 3 * Wb
    w_convfox = jnp.concatenate([w_mix_in[:, :o_cq], w_mix_in[:, o_fox:o_f]], axis=1).astype(BF16)
    w_cq = w_mix_in[:, o_cq:o_ckv].astype(BF16)
    w_ckv = w_mix_in[:, o_ckv:o_kr].astype(BF16)
    w_kr = w_mix_in[:, o_kr:o_fox]
    zpad = jnp.zeros((D, LANES - ROPE_DIM), F32)
    w_kra = jnp.concatenate([w_kr, zpad], axis=1).astype(BF16)
    w_krb = jnp.concatenate([-w_kr[:, half:], w_kr[:, :half], zpad], axis=1).astype(BF16)
    w_f = jnp.pad(w_mix_in[:, o_f:o_f + H], ((0, 0), (0, LANES - H))).astype(BF16)
    b_f = jnp.pad(fox_f_bias, (0, LANES - H)).reshape(1, LANES)

    uq = w_uq.reshape(q_lora, H, HEAD_DIM + ROPE_DIM)
    zq = jnp.zeros((q_lora, H, QK_PAD - HEAD_DIM - ROPE_DIM), F32)
    w_qa = jnp.concatenate([uq, zq], axis=2).reshape(q_lora, H * QK_PAD).astype(BF16)
    r1 = uq[:, :, HEAD_DIM:HEAD_DIM + half]
    r2 = uq[:, :, HEAD_DIM + half:]
    w_qb = jnp.concatenate([-r2, r1, zq], axis=2).reshape(q_lora, H * LANES).astype(BF16)

    ukv = w_ukv.reshape(kv_lora, H, 2 * HEAD_DIM)
    w_k = ukv[:, :, :HEAD_DIM].reshape(kv_lora, H * HEAD_DIM).astype(BF16)
    w_v = ukv[:, :, HEAD_DIM:].reshape(kv_lora, H * HEAD_DIM).astype(BF16)
    return dict(w_convfox=w_convfox, w_cq=w_cq, w_ckv=w_ckv, w_kra=w_kra, w_krb=w_krb,
                w_f=w_f, b_f=b_f, w_qa=w_qa, w_qb=w_qb, w_k=w_k, w_v=w_v)


def _mixer(h, B, S, D, rope_cos, rope_sin, mw, conv_w, q_norm_g, kv_norm_g,
           w_branch, w_merge_gate, b_merge_gate, w_o):
    Wb = D // 4
    H = Wb // HEAD_DIM
    u = _mm(h, mw["w_convfox"], BF16, tn_pref=512, name="mix_in")
    o_a = _conv_call(u, conv_w, B, S, Wb)

    q_mla = _mla_q_call(h, mw["w_cq"], q_norm_g, mw["w_qa"], mw["w_qb"], rope_cos, rope_sin, H)
    k_mla, v_mla = _mla_kv_call(h, mw["w_ckv"], kv_norm_g, mw["w_k"], mw["w_v"],
                                mw["w_kra"], mw["w_krb"], rope_cos, rope_sin, H)
    o_b = _flash_call(q_mla, k_mla, v_mla, B, S, H, (HEAD_DIM + ROPE_DIM) ** -0.5, name="flash_mla")

    fox_scale = HEAD_DIM ** -0.5
    q_fox, k_fox = _fox_prep_call(h, u, mw["w_f"], mw["b_f"], B, S, H, 3, 4, fox_scale)
    o_c = _flash_call(q_fox, k_fox, u, B, S, H, fox_scale, v_blk0=5 * H, name="flash_fox")

    merged = _merge_call(h, w_merge_gate.astype(BF16), b_merge_gate, (o_a, o_b, o_c),
                         w_branch.astype(BF16))
    return _mm(merged, w_o.astype(BF16), BF16, name="mix_out")


def _moe(x, g_pre, scale, shift, g_post, gate, w_router, b_router, w_gate_up, b_gate_up,
         w_down, b_down):
    B, S, D = x.shape
    N = B * S
    E = w_router.shape[1]
    h, top_e, top_p = _norm_router_call(x, g_pre, scale, shift, w_router, b_router)
    rank, counts8 = _rank_call(top_e)

    counts = counts8[0, :E]
    padded = (counts + EXPERT_ROWS - 1) // EXPERT_ROWS * EXPERT_ROWS
    pad_end = jnp.cumsum(padded)
    pad_start = pad_end - padded
    e_flat = top_e[:, :TOP_K].reshape(-1)
    dest = pad_start[e_flat] + rank[:, :TOP_K].reshape(-1)
    n_blocks = (N * TOP_K + E * (EXPERT_ROWS - 1) + EXPERT_ROWS - 1) // EXPERT_ROWS
    n_rows = n_blocks * EXPERT_ROWS
    block_start = jnp.arange(n_blocks, dtype=jnp.int32) * EXPERT_ROWS
    block_e = jnp.minimum(
        jnp.sum((pad_end[None, :] <= block_start[:, None]).astype(jnp.int32), axis=1), E - 1)
    n_used = (pad_end[-1:] // EXPERT_ROWS).astype(jnp.int32)

    x_grouped = _dispatch_call(h.reshape(N, D // 2), dest.astype(jnp.int32),
                               (pad_start + counts).astype(jnp.int32),
                               pad_end.astype(jnp.int32), n_used, n_rows)
    y_grouped = _expert_call(x_grouped, block_e, n_used, w_gate_up.astype(BF16), b_gate_up,
                             w_down.astype(BF16), b_down)
    return _combine_post_call(x, y_grouped, dest.astype(jnp.int32), top_p, g_post, gate)


def kernel(x, c, positions, w_ada, b_ada, g_mix_pre, g_mix_post, g_ffn_pre, g_ffn_post, w_mix_in, conv_w, q_norm_g, kv_norm_g, w_uq, w_ukv, fox_f_bias, w_branch, w_merge_gate, b_merge_gate, w_o, w_router, b_router, w_gate_up, b_gate_up, w_down, b_down):
    B, S, D = x.shape
    L = w_ada.shape[0]
    N = B * S

    c8 = jnp.pad(c, ((0, 8 - B), (0, 0)))
    ada = _ada(c8, w_ada, b_ada)[:, :B].reshape(L, B, N_ADA, 1, D)

    half = ROPE_DIM // 2
    inv_freq = ROPE_THETA ** (-jnp.arange(half, dtype=F32) / half)
    ang = positions.astype(F32).reshape(N, 1) * inv_freq
    zer = jnp.zeros((N, LANES - ROPE_DIM), F32)
    rope_cos = jnp.concatenate([jnp.cos(ang), jnp.cos(ang), zer], axis=1)
    rope_sin = jnp.concatenate([jnp.sin(ang), jnp.sin(ang), zer], axis=1)

    for l in range(L):
        shift_m, scale_m, gate_m = ada[l, :, 0], ada[l, :, 1], ada[l, :, 2]
        shift_f, scale_f, gate_f = ada[l, :, 3], ada[l, :, 4], ada[l, :, 5]

        h = _norm_mod_call(x, g_mix_pre[l], scale_m, shift_m).reshape(N, D)
        mw = _mixer_weights(D, w_mix_in[l], w_uq[l], w_ukv[l], fox_f_bias[l])
        y = _mixer(h, B, S, D, rope_cos, rope_sin, mw, conv_w[l], q_norm_g[l], kv_norm_g[l],
                   w_branch[l], w_merge_gate[l], b_merge_gate[l], w_o[l])
        x = _post_call(x, y, g_mix_post[l], gate_m)

        x = _moe(x, g_ffn_pre[l], scale_f, shift_f, g_ffn_post[l], gate_f, w_router[l],
                 b_router[l], w_gate_up[l], b_gate_up[l], w_down[l], b_down[l])
    return x
```

```python
import functools

import jax
import jax.numpy as jnp
from jax import lax
from jax.experimental import pallas as pl
from jax.experimental.pallas import tpu as pltpu

NORM_EPS = 1e-6
ROPE_THETA = 10000.0
HEAD_DIM = 128
ROPE_DIM = 64
CONV_K = 3
TOP_K = 4
SWIGLU_ALPHA = 1.702
SWIGLU_LIMIT = 7.0
N_ADA = 6
LANES = 128
QK_PAD = 2 * LANES
EXPERT_ROWS = 256
MIB = 1024 * 1024

F32 = jnp.float32
BF16 = jnp.bfloat16


def _cparams(sem, vmem_mib=48):
    return pltpu.CompilerParams(dimension_semantics=sem, vmem_limit_bytes=vmem_mib * MIB)


def _tile(n, pref):
    t = min(pref, n)
    while n % t:
        t //= 2
    return t


def _pack_bf16_pairs(x):
    n = x.shape[1] // 2
    lo = lax.bitcast_convert_type(x[:, :n].astype(BF16).astype(F32), jnp.uint32)
    hi = lax.bitcast_convert_type(x[:, n:].astype(BF16).astype(F32), jnp.uint32)
    return hi | (lo >> 16)


def _unpack_bf16_pairs(w):
    lo = lax.bitcast_convert_type(w << 16, F32)
    hi = lax.bitcast_convert_type(w & jnp.uint32(0xFFFF0000), F32)
    return jnp.concatenate([lo, hi], axis=1)


def _split3(x):
    hi = x.astype(BF16)
    r1 = x - hi.astype(F32)
    mid = r1.astype(BF16)
    lo = (r1 - mid.astype(F32)).astype(BF16)
    return hi, mid, lo


def _ada_kernel(c_ref, w_ref, b_ref, o_ref):
    c = c_ref[...]
    c_act = (c * jax.nn.sigmoid(c)).astype(BF16)
    acc = jnp.dot(c_act, w_ref[0].astype(BF16), preferred_element_type=F32)
    o_ref[0] = acc + b_ref[0]


def _ada(c8, w_ada, b_ada):
    L, D, N = w_ada.shape
    tn = _tile(N, 1024)
    return pl.pallas_call(
        _ada_kernel,
        grid=(L, N // tn),
        in_specs=[
            pl.BlockSpec((8, D), lambda l, j: (0, 0)),
            pl.BlockSpec((1, D, tn), lambda l, j: (l, 0, j)),
            pl.BlockSpec((1, 1, tn), lambda l, j: (l, 0, j)),
        ],
        out_specs=pl.BlockSpec((1, 8, tn), lambda l, j: (l, 0, j)),
        out_shape=jax.ShapeDtypeStruct((L, 8, N), F32),
        compiler_params=_cparams(("parallel", "parallel"), 56),
        name="ada",
    )(c8, w_ada, b_ada.reshape(L, 1, N))


def _norm_mod(x, g, scale, shift):
    xf = x
    y = xf * lax.rsqrt(jnp.mean(xf * xf, axis=-1, keepdims=True) + NORM_EPS)
    return (y * g) * (1.0 + scale) + shift


def _norm_mod_kernel(x_ref, g_ref, sc_ref, sh_ref, o_ref):
    o_ref[0] = _norm_mod(x_ref[0], g_ref[...], sc_ref[0], sh_ref[0]).astype(o_ref.dtype)


def _norm_mod_call(x, g, scale, shift):
    B, S, D = x.shape
    ts = _tile(S, 256)
    return pl.pallas_call(
        _norm_mod_kernel,
        grid=(B, S // ts),
        in_specs=[
            pl.BlockSpec((1, ts, D), lambda b, i: (b, i, 0)),
            pl.BlockSpec((1, D), lambda b, i: (0, 0)),
            pl.BlockSpec((1, 1, D), lambda b, i: (b, 0, 0)),
            pl.BlockSpec((1, 1, D), lambda b, i: (b, 0, 0)),
        ],
        out_specs=pl.BlockSpec((1, ts, D), lambda b, i: (b, i, 0)),
        out_shape=jax.ShapeDtypeStruct((B, S, D), BF16),
        compiler_params=_cparams(("parallel", "parallel")),
        name="norm_mod",
    )(x, g.reshape(1, D), scale, shift)


def _norm_router_kernel(n_experts, x_ref, g_ref, sc_ref, sh_ref, wr_hi_ref, wr_lo_ref, br_ref,
                        h_ref, e_ref, p_ref):
    h = _norm_mod(x_ref[0], g_ref[...], sc_ref[0], sh_ref[0])
    h_ref[0] = _pack_bf16_pairs(h)
    h_hi = h.astype(BF16)
    h_lo = (h - h_hi.astype(F32)).astype(BF16)
    logits = (jnp.dot(h_hi, wr_hi_ref[...], preferred_element_type=F32)
              + jnp.dot(h_hi, wr_lo_ref[...], preferred_element_type=F32)
              + jnp.dot(h_lo, wr_hi_ref[...], preferred_element_type=F32)) + br_ref[...]
    lane = lax.broadcasted_iota(jnp.int32, logits.shape, 1)
    vals = jnp.where(lane < n_experts, logits, -jnp.inf)
    tops, idxs = [], []
    for _ in range(TOP_K):
        m = jnp.max(vals, axis=-1, keepdims=True)
        idx = jnp.min(jnp.where(vals == m, lane, LANES), axis=-1, keepdims=True)
        tops.append(m)
        idxs.append(idx)
        vals = jnp.where(lane == idx, -jnp.inf, vals)
    exps = [jnp.exp(t - tops[0]) for t in tops]
    denom = exps[0]
    for e in exps[1:]:
        denom = denom + e
    e_out = jnp.zeros(logits.shape, jnp.int32)
    p_out = jnp.zeros(logits.shape, F32)
    for k in range(TOP_K):
        e_out = jnp.where(lane == k, idxs[k], e_out)
        p_out = jnp.where(lane == k, exps[k] / denom, p_out)
    e_ref[...] = e_out
    p_ref[...] = p_out


def _norm_router_call(x, g, scale, shift, w_router, b_router):
    B, S, D = x.shape
    E = w_router.shape[1]
    ts = _tile(S, 256)
    nb = S // ts
    wr = jnp.pad(w_router, ((0, 0), (0, LANES - E)))
    wr_hi = wr.astype(BF16)
    wr_lo = (wr - wr_hi.astype(F32)).astype(BF16)
    br = jnp.pad(b_router, (0, LANES - E)).reshape(1, LANES)
    return pl.pallas_call(
        functools.partial(_norm_router_kernel, E),
        grid=(B, nb),
        in_specs=[
            pl.BlockSpec((1, ts, D), lambda b, i: (b, i, 0)),
            pl.BlockSpec((1, D), lambda b, i: (0, 0)),
            pl.BlockSpec((1, 1, D), lambda b, i: (b, 0, 0)),
            pl.BlockSpec((1, 1, D), lambda b, i: (b, 0, 0)),
            pl.BlockSpec((D, LANES), lambda b, i: (0, 0)),
            pl.BlockSpec((D, LANES), lambda b, i: (0, 0)),
            pl.BlockSpec((1, LANES), lambda b, i: (0, 0)),
        ],
        out_specs=[
            pl.BlockSpec((1, ts, D // 2), lambda b, i: (b, i, 0)),
            pl.BlockSpec((ts, LANES), lambda b, i: (b * nb + i, 0)),
            pl.BlockSpec((ts, LANES), lambda b, i: (b * nb + i, 0)),
        ],
        out_shape=[
            jax.ShapeDtypeStruct((B, S, D // 2), jnp.uint32),
            jax.ShapeDtypeStruct((B * S, LANES), jnp.int32),
            jax.ShapeDtypeStruct((B * S, LANES), F32),
        ],
        compiler_params=_cparams(("parallel", "parallel")),
        name="norm_router",
    )(x, g.reshape(1, D), scale, shift, wr_hi, wr_lo, br)


def _mm_kernel(a_ref, w_ref, o_ref):
    o_ref[...] = jnp.dot(a_ref[...], w_ref[0], preferred_element_type=F32).astype(o_ref.dtype)


def _mm(a, w, layer, out_dtype, tm_pref=1024, tn_pref=512, name="mm"):
    M, K = a.shape
    N = w.shape[2]
    tm = _tile(M, tm_pref)
    tn = _tile(N, tn_pref)
    return pl.pallas_call(
        _mm_kernel,
        grid=(M // tm, N // tn),
        in_specs=[
            pl.BlockSpec((tm, K), lambda i, j: (i, 0)),
            pl.BlockSpec((1, K, tn), lambda i, j: (layer, 0, j)),
        ],
        out_specs=pl.BlockSpec((tm, tn), lambda i, j: (i, j)),
        out_shape=jax.ShapeDtypeStruct((M, N), out_dtype),
        compiler_params=_cparams(("parallel", "parallel"), 56),
        name=name,
    )(a, w)


def _conv_kernel(b_ref, c_ref, h_ref, w_ref, o_ref):
    g = c_ref[...].astype(F32) * h_ref[...].astype(F32)
    row = lax.broadcasted_iota(jnp.int32, g.shape, 0)
    g1 = jnp.where(row >= 1, pltpu.roll(g, 1, 0), 0.0)
    g2 = jnp.where(row >= 2, pltpu.roll(g, 2, 0), 0.0)
    w = w_ref[...]
    conv = w[0:1] * g2 + w[1:2] * g1 + w[2:3] * g
    o_ref[...] = (b_ref[...].astype(F32) * conv).astype(o_ref.dtype)


def _conv_call(u, conv_w, B, S, C):
    tc = _tile(C, 256)
    nc = C // tc
    w8 = jnp.pad(conv_w, ((0, 8 - CONV_K), (0, 0)))
    return pl.pallas_call(
        _conv_kernel,
        grid=(B, nc),
        in_specs=[
            pl.BlockSpec((S, tc), lambda b, j: (b, j)),
            pl.BlockSpec((S, tc), lambda b, j: (b, nc + j)),
            pl.BlockSpec((S, tc), lambda b, j: (b, 2 * nc + j)),
            pl.BlockSpec((8, tc), lambda b, j: (0, j)),
        ],
        out_specs=pl.BlockSpec((S, tc), lambda b, j: (b, j)),
        out_shape=jax.ShapeDtypeStruct((B * S, C), BF16),
        compiler_params=_cparams(("parallel", "parallel")),
        name="gated_conv",
    )(u, u, u, w8)


def _rms(x, g):
    return x * lax.rsqrt(jnp.mean(x * x, axis=-1, keepdims=True) + NORM_EPS) * g


def _mla_q_kernel(n_heads, h_ref, wc_ref, g_ref, wa_ref, wb_ref, cos_ref, sin_ref, q_ref):
    c_q = jnp.dot(h_ref[...], wc_ref[...], preferred_element_type=F32)
    n = _rms(c_q, g_ref[...]).astype(BF16)
    a = jnp.dot(n, wa_ref[...], preferred_element_type=F32)
    bm = jnp.dot(n, wb_ref[...], preferred_element_type=F32)
    cos = cos_ref[...]
    sin = sin_ref[...]
    for hd in range(n_heads):
        o = hd * QK_PAD
        q_ref[:, o:o + LANES] = a[:, o:o + LANES].astype(q_ref.dtype)
        rot = a[:, o + LANES:o + QK_PAD] * cos + bm[:, hd * LANES:(hd + 1) * LANES] * sin
        q_ref[:, o + LANES:o + QK_PAD] = rot.astype(q_ref.dtype)


def _mla_q_call(h, w_cq, q_norm_g, w_a, w_b, cos_t, sin_t, n_heads):
    M, D = h.shape
    R = w_cq.shape[1]
    tm = _tile(M, 512)
    const = lambda i: (0, 0)
    return pl.pallas_call(
        functools.partial(_mla_q_kernel, n_heads),
        grid=(M // tm,),
        in_specs=[
            pl.BlockSpec((tm, D), lambda i: (i, 0)),
            pl.BlockSpec((D, R), const),
            pl.BlockSpec((1, R), const),
            pl.BlockSpec((R, n_heads * QK_PAD), const),
            pl.BlockSpec((R, n_heads * LANES), const),
            pl.BlockSpec((tm, LANES), lambda i: (i, 0)),
            pl.BlockSpec((tm, LANES), lambda i: (i, 0)),
        ],
        out_specs=pl.BlockSpec((tm, n_heads * QK_PAD), lambda i: (i, 0)),
        out_shape=jax.ShapeDtypeStruct((M, n_heads * QK_PAD), BF16),
        compiler_params=_cparams(("parallel",), 56),
        name="mla_q",
    )(h, w_cq, q_norm_g.reshape(1, R), w_a, w_b, cos_t, sin_t)


def _mla_kv_kernel(n_heads, h_ref, wc_ref, g_ref, wk_ref, wv_ref, wra_ref, wrb_ref,
                   cos_ref, sin_ref, k_ref, v_ref):
    hb = h_ref[...]
    c_kv = jnp.dot(hb, wc_ref[...], preferred_element_type=F32)
    n = _rms(c_kv, g_ref[...]).astype(BF16)
    k_nope = jnp.dot(n, wk_ref[...], preferred_element_type=F32)
    v_ref[...] = jnp.dot(n, wv_ref[...], preferred_element_type=F32).astype(v_ref.dtype)
    ra = jnp.dot(hb, wra_ref[...], preferred_element_type=F32)
    rb = jnp.dot(hb, wrb_ref[...], preferred_element_type=F32)
    k_rope = (ra * cos_ref[...] + rb * sin_ref[...]).astype(k_ref.dtype)
    for hd in range(n_heads):
        o = hd * QK_PAD
        k_ref[:, o:o + LANES] = k_nope[:, hd * LANES:(hd + 1) * LANES].astype(k_ref.dtype)
        k_ref[:, o + LANES:o + QK_PAD] = k_rope


def _mla_kv_call(h, w_ckv, kv_norm_g, w_k, w_v, w_ra, w_rb, cos_t, sin_t, n_heads):
    M, D = h.shape
    R = w_ckv.shape[1]
    tm = _tile(M, 512)
    const = lambda i: (0, 0)
    return pl.pallas_call(
        functools.partial(_mla_kv_kernel, n_heads),
        grid=(M // tm,),
        in_specs=[
            pl.BlockSpec((tm, D), lambda i: (i, 0)),
            pl.BlockSpec((D, R), const),
            pl.BlockSpec((1, R), const),
            pl.BlockSpec((R, n_heads * LANES), const),
            pl.BlockSpec((R, n_heads * LANES), const),
            pl.BlockSpec((D, LANES), const),
            pl.BlockSpec((D, LANES), const),
            pl.BlockSpec((tm, LANES), lambda i: (i, 0)),
            pl.BlockSpec((tm, LANES), lambda i: (i, 0)),
        ],
        out_specs=[
            pl.BlockSpec((tm, n_heads * QK_PAD), lambda i: (i, 0)),
            pl.BlockSpec((tm, n_heads * LANES), lambda i: (i, 0)),
        ],
        out_shape=[
            jax.ShapeDtypeStruct((M, n_heads * QK_PAD), BF16),
            jax.ShapeDtypeStruct((M, n_heads * LANES), BF16),
        ],
        compiler_params=_cparams(("parallel",), 56),
        name="mla_kv",
    )(h, w_ckv, kv_norm_g.reshape(1, R), w_k, w_v, w_ra, w_rb, cos_t, sin_t)


def _fox_prep_kernel(n_heads, inv_scale, h_ref, wf_ref, bf_ref, q_ref, k_ref,
                     qo_ref, ko_ref, carry_ref):
    @pl.when(pl.program_id(1) == 0)
    def _():
        carry_ref[...] = jnp.zeros_like(carry_ref)

    f = jnp.dot(h_ref[...], wf_ref[...], preferred_element_type=F32) + bf_ref[...]
    log_f = jax.nn.log_sigmoid(f)
    ts = log_f.shape[0]
    row = lax.broadcasted_iota(jnp.int32, (ts, ts), 0)
    col = lax.broadcasted_iota(jnp.int32, (ts, ts), 1)
    tri = jnp.where(col <= row, 1.0, 0.0).astype(BF16)
    cum = carry_ref[0:1, :]
    for part in _split3(log_f):
        cum = cum + jnp.dot(tri, part, preferred_element_type=F32)
    carry_ref[...] = jnp.broadcast_to(cum[ts - 1:ts, :], carry_ref.shape)

    lane = lax.broadcasted_iota(jnp.int32, (ts, LANES), 1)
    cum_s = cum * inv_scale
    for hd in range(n_heads):
        colv = jnp.broadcast_to(cum_s[:, hd:hd + 1], (ts, LANES))
        hi, mid, lo = (part.astype(F32) for part in _split3(colv))
        q_aug = jnp.where(lane == 0, hi, jnp.where(lane == 1, mid, jnp.where(
            lane == 2, lo, jnp.where(lane < 6, 1.0, 0.0))))
        k_aug = jnp.where(lane < 3, 1.0, jnp.where(lane == 3, -hi, jnp.where(
            lane == 4, -mid, jnp.where(lane == 5, -lo, 0.0))))
        o = hd * QK_PAD
        qo_ref[:, o:o + LANES] = q_ref[:, hd * LANES:(hd + 1) * LANES]
        qo_ref[:, o + LANES:o + QK_PAD] = q_aug.astype(BF16)
        ko_ref[:, o:o + LANES] = k_ref[:, hd * LANES:(hd + 1) * LANES]
        ko_ref[:, o + LANES:o + QK_PAD] = k_aug.astype(BF16)


def _fox_prep_call(h, u, w_f, b_f, B, S, n_heads, q_blk, k_blk, scale):
    M, D = h.shape
    W = n_heads * LANES
    ts = _tile(S, 512)
    nb = S // ts
    row = lambda b, i: (b * nb + i, 0)
    return pl.pallas_call(
        functools.partial(_fox_prep_kernel, n_heads, 1.0 / scale),
        grid=(B, nb),
        in_specs=[
            pl.BlockSpec((ts, D), row),
            pl.BlockSpec((D, LANES), lambda b, i: (0, 0)),
            pl.BlockSpec((1, LANES), lambda b, i: (0, 0)),
            pl.BlockSpec((ts, W), lambda b, i: (b * nb + i, q_blk)),
            pl.BlockSpec((ts, W), lambda b, i: (b * nb + i, k_blk)),
        ],
        out_specs=[
            pl.BlockSpec((ts, n_heads * QK_PAD), row),
            pl.BlockSpec((ts, n_heads * QK_PAD), row),
        ],
        out_shape=[
            jax.ShapeDtypeStruct((M, n_heads * QK_PAD), BF16),
            jax.ShapeDtypeStruct((M, n_heads * QK_PAD), BF16),
        ],
        scratch_shapes=[pltpu.VMEM((8, LANES), F32)],
        compiler_params=_cparams(("parallel", "arbitrary")),
        name="fox_prep",
    )(h, w_f, b_f, u, u)


def _flash_kernel(scale_log2, t, hp, q_ref, k_ref, v_ref, o_ref, m_ref, l_ref, acc_ref):
    i = pl.program_id(2)
    m_ref[...] = jnp.full_like(m_ref, -jnp.inf)
    l_ref[...] = jnp.zeros_like(l_ref)
    acc_ref[...] = jnp.zeros_like(acc_ref)

    def tile(j, masked):
        off = pl.multiple_of(j * t, t)
        for hh in range(hp):
            q = q_ref[:, hh * QK_PAD:(hh + 1) * QK_PAD]
            k = k_ref[pl.ds(off, t), hh * QK_PAD:(hh + 1) * QK_PAD]
            v = v_ref[pl.ds(off, t), hh * HEAD_DIM:(hh + 1) * HEAD_DIM]
            s = lax.dot_general(q, k, (((1,), (1,)), ((), ())),
                                preferred_element_type=F32) * scale_log2
            if masked:
                row = lax.broadcasted_iota(jnp.int32, s.shape, 0)
                col = lax.broadcasted_iota(jnp.int32, s.shape, 1)
                s = jnp.where(col <= row, s, -jnp.inf)
            parts = [s[:, c * LANES:(c + 1) * LANES] for c in range(t // LANES)]
            part_max = parts[0]
            for part in parts[1:]:
                part_max = jnp.maximum(part_max, part)
            m_prev = m_ref[hh]
            m_new = jnp.maximum(m_prev, jnp.max(part_max, axis=-1, keepdims=True))
            alpha = jnp.exp2(m_prev - m_new)
            probs = [jnp.exp2(part - m_new) for part in parts]
            l_part = probs[0]
            for pr in probs[1:]:
                l_part = l_part + pr
            l_ref[hh] = alpha * l_ref[hh] + l_part
            p = jnp.concatenate(probs, axis=1).astype(v.dtype)
            acc_ref[hh] = alpha * acc_ref[hh] + jnp.dot(p, v, preferred_element_type=F32)
            m_ref[hh] = m_new

    def body(j, c):
        tile(j, False)
        return c

    lax.fori_loop(0, i, body, 0)
    tile(i, True)
    for hh in range(hp):
        l_row = jnp.sum(l_ref[hh], axis=-1, keepdims=True)
        o_ref[:, hh * HEAD_DIM:(hh + 1) * HEAD_DIM] = (acc_ref[hh] / l_row).astype(o_ref.dtype)


def _flash_call(q, k, v, B, S, n_heads, scale, v_blk0=0, name="flash"):
    M = q.shape[0]
    t = _tile(S, 1024)
    nq = S // t
    hp = 2
    assert n_heads % hp == 0 and v_blk0 % hp == 0
    return pl.pallas_call(
        functools.partial(_flash_kernel, scale * 1.4426950408889634, t, hp),
        grid=(B, n_heads // hp, nq),
        in_specs=[
            pl.BlockSpec((t, hp * QK_PAD), lambda b, h, i: (b * nq + i, h)),
            pl.BlockSpec((S, hp * QK_PAD), lambda b, h, i: (b, h)),
            pl.BlockSpec((S, hp * HEAD_DIM), lambda b, h, i: (b, v_blk0 // hp + h)),
        ],
        out_specs=pl.BlockSpec((t, hp * HEAD_DIM), lambda b, h, i: (b * nq + i, h)),
        out_shape=jax.ShapeDtypeStruct((M, n_heads * HEAD_DIM), BF16),
        scratch_shapes=[
            pltpu.VMEM((hp, t, LANES), F32),
            pltpu.VMEM((hp, t, LANES), F32),
            pltpu.VMEM((hp, t, HEAD_DIM), F32),
        ],
        compiler_params=_cparams(("parallel", "parallel", "arbitrary")),
        name=name,
    )(q, k, v)


def _merge_kernel(n_br, h_ref, wg_ref, bg_ref, *rest):
    br_refs = rest[:n_br]
    wb_ref = rest[n_br]
    o_ref = rest[n_br + 1]
    hb = h_ref[...]
    merged = None
    for n in range(n_br):
        gate = jax.nn.sigmoid(jnp.dot(hb, wg_ref[0, n], preferred_element_type=F32) + bg_ref[n])
        proj = jnp.dot(br_refs[n][...], wb_ref[0, n], preferred_element_type=F32)
        merged = gate * proj if merged is None else merged + gate * proj
    o_ref[...] = merged.astype(o_ref.dtype)


def _merge_call(h, w_gate, layer, b_gate, branches, w_branch):
    M, D = h.shape
    _, n_br, W, _ = w_branch.shape
    tm = _tile(M, 1024)
    tn = _tile(D, 256)
    in_specs = [
        pl.BlockSpec((tm, D), lambda i, j: (i, 0)),
        pl.BlockSpec((1, n_br, D, tn), lambda i, j: (layer, 0, 0, j)),
        pl.BlockSpec((n_br, 1, tn), lambda i, j: (0, 0, j)),
    ]
    in_specs += [pl.BlockSpec((tm, W), lambda i, j: (i, 0)) for _ in range(n_br)]
    in_specs += [pl.BlockSpec((1, n_br, W, tn), lambda i, j: (layer, 0, 0, j))]
    return pl.pallas_call(
        functools.partial(_merge_kernel, n_br),
        grid=(M // tm, D // tn),
        in_specs=in_specs,
        out_specs=pl.BlockSpec((tm, tn), lambda i, j: (i, j)),
        out_shape=jax.ShapeDtypeStruct((M, D), BF16),
        compiler_params=_cparams(("parallel", "parallel"), 56),
        name="gated_merge",
    )(h, w_gate, b_gate.reshape(n_br, 1, D), *branches, w_branch)


def _post_kernel(x_ref, y_ref, g_ref, gate_ref, o_ref):
    o_ref[0] = x_ref[0] + gate_ref[0] * _rms(y_ref[0].astype(F32), g_ref[...])


def _post_call(x, y, g, gate):
    B, S, D = x.shape
    ts = _tile(S, 256)
    blk = pl.BlockSpec((1, ts, D), lambda b, i: (b, i, 0))
    return pl.pallas_call(
        _post_kernel,
        grid=(B, S // ts),
        in_specs=[blk, blk, pl.BlockSpec((1, D), lambda b, i: (0, 0)),
                  pl.BlockSpec((1, 1, D), lambda b, i: (b, 0, 0))],
        out_specs=blk,
        out_shape=jax.ShapeDtypeStruct((B, S, D), F32),
        compiler_params=_cparams(("parallel", "parallel")),
        name="post_residual",
    )(x, y.reshape(B, S, D), g.reshape(1, D), gate)


def _combine_post_kernel(ts, n_steps, dest_ref, x_ref, y_hbm, p_ref, g_ref, gate_ref, o_ref,
                         ybuf, sems):
    s = pl.program_id(0)

    def issue(step, slot):
        def tok_body(r, c):
            t = step * ts + r
            for k in range(TOP_K):
                pltpu.make_async_copy(y_hbm.at[pl.ds(dest_ref[t * TOP_K + k], 1)],
                                      ybuf.at[slot, k, pl.ds(r, 1)], sems.at[slot]).start()
            return c
        lax.fori_loop(0, ts, tok_body, 0)

    @pl.when(s == 0)
    def _():
        issue(0, 0)

    @pl.when(s + 1 < n_steps)
    def _():
        issue(s + 1, (s + 1) % 2)

    slot = s % 2
    pltpu.make_async_copy(ybuf.at[slot], ybuf.at[slot], sems.at[slot]).wait()
    p = p_ref[...]
    y = _unpack_bf16_pairs(ybuf[slot, 0]) * p[:, 0:1]
    for k in range(1, TOP_K):
        y = y + _unpack_bf16_pairs(ybuf[slot, k]) * p[:, k:k + 1]
    o_ref[0] = x_ref[0] + gate_ref[0] * _rms(y, g_ref[...])


def _combine_post_call(x, y_grouped, dest_flat, top_p, g, gate):
    B, S, D = x.shape
    ts = _tile(S, 128)
    nb = S // ts
    blk = pl.BlockSpec((1, ts, D), lambda s, d: (s // nb, s % nb, 0))
    grid_spec = pltpu.PrefetchScalarGridSpec(
        num_scalar_prefetch=1,
        grid=(B * nb,),
        in_specs=[blk,
                  pl.BlockSpec(memory_space=pl.ANY),
                  pl.BlockSpec((ts, LANES), lambda s, d: (s, 0)),
                  pl.BlockSpec((1, D), lambda s, d: (0, 0)),
                  pl.BlockSpec((1, 1, D), lambda s, d: (s // nb, 0, 0))],
        out_specs=blk,
        scratch_shapes=[pltpu.VMEM((2, TOP_K, ts, D // 2), jnp.uint32),
                        pltpu.SemaphoreType.DMA((2,))],
    )
    return pl.pallas_call(
        functools.partial(_combine_post_kernel, ts, B * nb),
        grid_spec=grid_spec,
        out_shape=jax.ShapeDtypeStruct((B, S, D), F32),
        compiler_params=_cparams(("arbitrary",)),
        name="combine_post",
    )(dest_flat, x, y_grouped, top_p, g.reshape(1, D), gate)


def _rank_kernel(e_ref, rank_ref, cnt_ref, carry_ref):
    @pl.when(pl.program_id(0) == 0)
    def _():
        carry_ref[...] = jnp.zeros_like(carry_ref)

    e = e_ref[...]
    tb = e.shape[0]
    lane = lax.broadcasted_iota(jnp.int32, e.shape, 1)
    sel = [jnp.broadcast_to(e[:, k:k + 1], e.shape) == lane for k in range(TOP_K)]
    onehot = jnp.zeros(e.shape, F32)
    for s in sel:
        onehot = onehot + jnp.where(s, 1.0, 0.0)
    row = lax.broadcasted_iota(jnp.int32, (tb, tb), 0)
    col = lax.broadcasted_iota(jnp.int32, (tb, tb), 1)
    strict = jnp.where(col < row, 1.0, 0.0).astype(BF16)
    before = carry_ref[0:1, :] + jnp.dot(strict, onehot.astype(BF16), preferred_element_type=F32)
    rank = jnp.zeros(e.shape, F32)
    for k in range(TOP_K):
        r_k = jnp.sum(jnp.where(sel[k], before, 0.0), axis=-1, keepdims=True)
        rank = jnp.where(lane == k, r_k, rank)
    rank_ref[...] = rank.astype(jnp.int32)
    total = carry_ref[0:1, :] + jnp.sum(onehot, axis=0, keepdims=True)
    carry_ref[...] = jnp.broadcast_to(total, carry_ref.shape)
    cnt_ref[...] = jnp.broadcast_to(total, cnt_ref.shape).astype(jnp.int32)


def _rank_call(top_e):
    N = top_e.shape[0]
    tb = _tile(N, 512)
    return pl.pallas_call(
        _rank_kernel,
        grid=(N // tb,),
        in_specs=[pl.BlockSpec((tb, LANES), lambda i: (i, 0))],
        out_specs=[pl.BlockSpec((tb, LANES), lambda i: (i, 0)),
                   pl.BlockSpec((8, LANES), lambda i: (0, 0))],
        out_shape=[jax.ShapeDtypeStruct((N, LANES), jnp.int32),
                   jax.ShapeDtypeStruct((8, LANES), jnp.int32)],
        scratch_shapes=[pltpu.VMEM((8, LANES), F32)],
        compiler_params=_cparams(("arbitrary",)),
        name="route_rank",
    )(top_e)


def _row_copy(src, dst, s_row, d_row, sem):
    return pltpu.make_async_copy(src.at[pl.ds(s_row, 1)], dst.at[pl.ds(d_row, 1)], sem)


def _rows_wait(ref, n, sem):
    pltpu.make_async_copy(ref.at[pl.ds(0, n)], ref.at[pl.ds(0, n)], sem).wait()


def _dispatch_kernel(tb, n_experts, n_blocks, dest_ref, zlo_ref, zhi_ref, nblk_ref,
                     h_ref, xg_ref, zero_ref, sem):
    i = pl.program_id(0)

    @pl.when(i == 0)
    def _():
        zero_ref[...] = jnp.zeros_like(zero_ref)

        def exp_body(e, carry):
            def issue(r, c2):
                _row_copy(zero_ref, xg_ref, 0, r, sem).start()
                return c2

            def wait(r, c2):
                _row_copy(zero_ref, xg_ref, 0, r, sem).wait()
                return c2

            lax.fori_loop(zlo_ref[e], zhi_ref[e], issue, 0)
            lax.fori_loop(zlo_ref[e], zhi_ref[e], wait, 0)
            return carry

        lax.fori_loop(0, n_experts, exp_body, 0)

        def tail_body(b, carry):
            cp = pltpu.make_async_copy(
                zero_ref,
                xg_ref.at[pl.ds(pl.multiple_of(b * EXPERT_ROWS, EXPERT_ROWS), EXPERT_ROWS)], sem)
            cp.start()
            cp.wait()
            return carry

        lax.fori_loop(nblk_ref[0], n_blocks, tail_body, 0)

    def tok_body(r, carry):
        t = i * tb + r
        for k in range(TOP_K):
            _row_copy(h_ref, xg_ref, r, dest_ref[t * TOP_K + k], sem).start()
        return carry

    lax.fori_loop(0, tb, tok_body, 0)
    _rows_wait(xg_ref, tb * TOP_K, sem)


def _dispatch_call(h, dest_flat, zlo, zhi, n_used, n_rows):
    N, W = h.shape
    E = zlo.shape[0]
    tb = _tile(N, 256)
    grid_spec = pltpu.PrefetchScalarGridSpec(
        num_scalar_prefetch=4,
        grid=(N // tb,),
        in_specs=[pl.BlockSpec((tb, W), lambda i, *_: (i, 0))],
        out_specs=pl.BlockSpec(memory_space=pl.ANY),
        scratch_shapes=[pltpu.VMEM((EXPERT_ROWS, W), h.dtype), pltpu.SemaphoreType.DMA(())],
    )
    return pl.pallas_call(
        functools.partial(_dispatch_kernel, tb, E, n_rows // EXPERT_ROWS),
        grid_spec=grid_spec,
        out_shape=jax.ShapeDtypeStruct((n_rows, W), h.dtype),
        compiler_params=pltpu.CompilerParams(dimension_semantics=("arbitrary",),
                                             has_side_effects=True),
        name="moe_dispatch",
    )(dest_flat, zlo, zhi, n_used, h)


def _expert_kernel(f_dim, be_ref, nu_ref, x_ref, wgu_ref, bgu_ref, wd_ref, bd_ref, o_ref):
    i = pl.program_id(0)

    @pl.when(i < nu_ref[0])
    def _():
        xb = _unpack_bf16_pairs(x_ref[...]).astype(BF16)
        gu = jnp.dot(xb, wgu_ref[0, 0], preferred_element_type=F32) + bgu_ref[0]
        g = jnp.minimum(gu[:, :f_dim], SWIGLU_LIMIT)
        up = jnp.clip(gu[:, f_dim:], -SWIGLU_LIMIT, SWIGLU_LIMIT)
        act = g * jax.nn.sigmoid(SWIGLU_ALPHA * g) * (up + 1.0)
        y = jnp.dot(act.astype(BF16), wd_ref[0, 0], preferred_element_type=F32) + bd_ref[0]
        o_ref[...] = _pack_bf16_pairs(y)

    @pl.when(i >= nu_ref[0])
    def _():
        o_ref[...] = jnp.zeros_like(o_ref)


def _expert_call(x_grouped, block_e, n_used, w_gu, layer, b_gu, w_d, b_d):
    _, E, D, F2 = w_gu.shape
    n_rows = x_grouped.shape[0]
    F = F2 // 2
    n_blocks = n_rows // EXPERT_ROWS
    grid_spec = pltpu.PrefetchScalarGridSpec(
        num_scalar_prefetch=2,
        grid=(n_blocks,),
        in_specs=[
            pl.BlockSpec((EXPERT_ROWS, D // 2), lambda i, be, nu: (jnp.minimum(i, nu[0] - 1), 0)),
            pl.BlockSpec((1, 1, D, F2), lambda i, be, nu: (layer, be[i], 0, 0)),
            pl.BlockSpec((1, 1, F2), lambda i, be, nu: (be[i], 0, 0)),
            pl.BlockSpec((1, 1, F, D), lambda i, be, nu: (layer, be[i], 0, 0)),
            pl.BlockSpec((1, 1, D), lambda i, be, nu: (be[i], 0, 0)),
        ],
        out_specs=pl.BlockSpec((EXPERT_ROWS, D // 2), lambda i, be, nu: (i, 0)),
    )
    return pl.pallas_call(
        functools.partial(_expert_kernel, F),
        grid_spec=grid_spec,
        out_shape=jax.ShapeDtypeStruct((n_rows, D // 2), jnp.uint32),
        compiler_params=_cparams(("arbitrary",), 56),
        name="experts",
    )(block_e, n_used, x_grouped, w_gu, b_gu.reshape(E, 1, F2), w_d, b_d.reshape(E, 1, D))


def _mixer_weights(D, w_mix_in, w_uq, w_ukv, fox_f_bias):
    Wb = D // 4
    H = Wb // HEAD_DIM
    q_lora = 3 * D // 16
    kv_lora = D // 8
    half = ROPE_DIM // 2
    o_cq = 3 * Wb
    o_ckv = o_cq + q_lora
    o_kr = o_ckv + kv_lora
    o_fox = o_kr + ROPE_DIM
    o_f = o_fox + 3 * Wb
    w_convfox = jnp.concatenate([w_mix_in[:, :o_cq], w_mix_in[:, o_fox:o_f]], axis=1)
    w_cq = w_mix_in[:, o_cq:o_ckv]
    w_ckv = w_mix_in[:, o_ckv:o_kr]
    w_kr = w_mix_in[:, o_kr:o_fox]
    zpad = jnp.zeros((D, LANES - ROPE_DIM), BF16)
    w_kra = jnp.concatenate([w_kr, zpad], axis=1)
    w_krb = jnp.concatenate([-w_kr[:, half:], w_kr[:, :half], zpad], axis=1)
    w_f = jnp.pad(w_mix_in[:, o_f:o_f + H], ((0, 0), (0, LANES - H)))
    b_f = jnp.pad(fox_f_bias, (0, LANES - H)).reshape(1, LANES)

    uq = w_uq.reshape(q_lora, H, HEAD_DIM + ROPE_DIM)
    zq = jnp.zeros((q_lora, H, QK_PAD - HEAD_DIM - ROPE_DIM), BF16)
    w_qa = jnp.concatenate([uq, zq], axis=2).reshape(q_lora, H * QK_PAD)
    r1 = uq[:, :, HEAD_DIM:HEAD_DIM + half]
    r2 = uq[:, :, HEAD_DIM + half:]
    w_qb = jnp.concatenate([-r2, r1, zq], axis=2).reshape(q_lora, H * LANES)

    ukv = w_ukv.reshape(kv_lora, H, 2 * HEAD_DIM)
    w_k = ukv[:, :, :HEAD_DIM].reshape(kv_lora, H * HEAD_DIM)
    w_v = ukv[:, :, HEAD_DIM:].reshape(kv_lora, H * HEAD_DIM)
    return dict(w_convfox=w_convfox, w_cq=w_cq, w_ckv=w_ckv, w_kra=w_kra, w_krb=w_krb,
                w_f=w_f, b_f=b_f, w_qa=w_qa, w_qb=w_qb, w_k=w_k, w_v=w_v)


def _mixer(h, B, S, D, rope_cos, rope_sin, mw, conv_w, q_norm_g, kv_norm_g,
           w_branch, w_merge_gate, layer, b_merge_gate, w_o):
    Wb = D // 4
    H = Wb // HEAD_DIM
    u = _mm(h, mw["w_convfox"][None], 0, BF16, name="mix_in")
    o_a = _conv_call(u, conv_w, B, S, Wb)

    q_mla = _mla_q_call(h, mw["w_cq"], q_norm_g, mw["w_qa"], mw["w_qb"], rope_cos, rope_sin, H)
    k_mla, v_mla = _mla_kv_call(h, mw["w_ckv"], kv_norm_g, mw["w_k"], mw["w_v"],
                                mw["w_kra"], mw["w_krb"], rope_cos, rope_sin, H)
    o_b = _flash_call(q_mla, k_mla, v_mla, B, S, H, (HEAD_DIM + ROPE_DIM) ** -0.5, name="flash_mla")

    fox_scale = HEAD_DIM ** -0.5
    q_fox, k_fox = _fox_prep_call(h, u, mw["w_f"], mw["b_f"], B, S, H, 3, 4, fox_scale)
    o_c = _flash_call(q_fox, k_fox, u, B, S, H, fox_scale, v_blk0=5 * H, name="flash_fox")

    merged = _merge_call(h, w_merge_gate, layer, b_merge_gate, (o_a, o_b, o_c), w_branch)
    return _mm(merged, w_o, layer, BF16, name="mix_out")


def _moe(x, g_pre, scale, shift, g_post, gate, w_router, b_router, w_gate_up, layer, b_gate_up,
         w_down, b_down):
    B, S, D = x.shape
    N = B * S
    E = w_router.shape[1]
    h, top_e, top_p = _norm_router_call(x, g_pre, scale, shift, w_router, b_router)
    rank, counts8 = _rank_call(top_e)

    counts = counts8[0, :E]
    padded = (counts + EXPERT_ROWS - 1) // EXPERT_ROWS * EXPERT_ROWS
    pad_end = jnp.cumsum(padded)
    pad_start = pad_end - padded
    e_flat = top_e[:, :TOP_K].reshape(-1)
    dest = pad_start[e_flat] + rank[:, :TOP_K].reshape(-1)
    n_blocks = (N * TOP_K + E * (EXPERT_ROWS - 1) + EXPERT_ROWS - 1) // EXPERT_ROWS
    n_rows = n_blocks * EXPERT_ROWS
    block_start = jnp.arange(n_blocks, dtype=jnp.int32) * EXPERT_ROWS
    block_e = jnp.minimum(
        jnp.sum((pad_end[None, :] <= block_start[:, None]).astype(jnp.int32), axis=1), E - 1)
    n_used = (pad_end[-1:] // EXPERT_ROWS).astype(jnp.int32)

    x_grouped = _dispatch_call(h.reshape(N, D // 2), dest.astype(jnp.int32),
                               (pad_start + counts).astype(jnp.int32),
                               pad_end.astype(jnp.int32), n_used, n_rows)
    y_grouped = _expert_call(x_grouped, block_e, n_used, w_gate_up, layer, b_gate_up,
                             w_down, b_down)
    return _combine_post_call(x, y_grouped, dest.astype(jnp.int32), top_p, g_post, gate)


def kernel(x, c, positions, w_ada, b_ada, g_mix_pre, g_mix_post, g_ffn_pre, g_ffn_post, w_mix_in, conv_w, q_norm_g, kv_norm_g, w_uq, w_ukv, fox_f_bias, w_branch, w_merge_gate, b_merge_gate, w_o, w_router, b_router, w_gate_up, b_gate_up, w_down, b_down):
    B, S, D = x.shape
    L = w_ada.shape[0]
    N = B * S

    c8 = jnp.pad(c, ((0, 8 - B), (0, 0)))
    ada = _ada(c8, w_ada, b_ada)[:, :B].reshape(L, B, N_ADA, 1, D)

    half = ROPE_DIM // 2
    inv_freq = ROPE_THETA ** (-jnp.arange(half, dtype=F32) / half)
    ang = positions.astype(F32).reshape(N, 1) * inv_freq
    zer = jnp.zeros((N, LANES - ROPE_DIM), F32)
    rope_cos = jnp.concatenate([jnp.cos(ang), jnp.cos(ang), zer], axis=1)
    rope_sin = jnp.concatenate([jnp.sin(ang), jnp.sin(ang), zer], axis=1)

    w_mix_bf, w_uq_bf, w_ukv_bf = (w.astype(BF16) for w in (w_mix_in, w_uq, w_ukv))
    w_branch_bf, w_gate_bf, w_o_bf = (w.astype(BF16) for w in (w_branch, w_merge_gate, w_o))
    w_gu_bf, w_down_bf = w_gate_up.astype(BF16), w_down.astype(BF16)

    for l in range(L):
        shift_m, scale_m, gate_m = ada[l, :, 0], ada[l, :, 1], ada[l, :, 2]
        shift_f, scale_f, gate_f = ada[l, :, 3], ada[l, :, 4], ada[l, :, 5]

        h = _norm_mod_call(x, g_mix_pre[l], scale_m, shift_m).reshape(N, D)
        mw = _mixer_weights(D, w_mix_bf[l], w_uq_bf[l], w_ukv_bf[l], fox_f_bias[l])
        y = _mixer(h, B, S, D, rope_cos, rope_sin, mw, conv_w[l], q_norm_g[l], kv_norm_g[l],
                   w_branch_bf, w_gate_bf, l, b_merge_gate[l], w_o_bf)
        x = _post_call(x, y, g_mix_post[l], gate_m)

        x = _moe(x, g_ffn_pre[l], scale_f, shift_f, g_ffn_post[l], gate_f, w_router[l],
                 b_router[l], w_gu_bf, l, b_gate_up[l], w_down_bf, b_down[l])
    return x
```

```python
import functools

import jax
import jax.numpy as jnp
from jax import lax
from jax.experimental import pallas as pl
from jax.experimental.pallas import tpu as pltpu

NORM_EPS = 1e-6
ROPE_THETA = 10000.0
HEAD_DIM = 128
ROPE_DIM = 64
CONV_K = 3
TOP_K = 4
SWIGLU_ALPHA = 1.702
SWIGLU_LIMIT = 7.0
N_ADA = 6
LANES = 128
QK_PAD = 2 * LANES
EXPERT_ROWS = 256
MIB = 1024 * 1024

F32 = jnp.float32
BF16 = jnp.bfloat16


def _cparams(sem, vmem_mib=48):
    return pltpu.CompilerParams(dimension_semantics=sem, vmem_limit_bytes=vmem_mib * MIB)


def _tile(n, pref):
    t = min(pref, n)
    while n % t:
        t //= 2
    return t


def _pack_bf16_pairs(x):
    n = x.shape[1] // 2
    lo = lax.bitcast_convert_type(x[:, :n].astype(BF16).astype(F32), jnp.uint32)
    hi = lax.bitcast_convert_type(x[:, n:].astype(BF16).astype(F32), jnp.uint32)
    return hi | (lo >> 16)


def _unpack_bf16_pairs(w):
    lo = lax.bitcast_convert_type(w << 16, F32)
    hi = lax.bitcast_convert_type(w & jnp.uint32(0xFFFF0000), F32)
    return jnp.concatenate([lo, hi], axis=1)


def _split3(x):
    hi = x.astype(BF16)
    r1 = x - hi.astype(F32)
    mid = r1.astype(BF16)
    lo = (r1 - mid.astype(F32)).astype(BF16)
    return hi, mid, lo


def _ada_kernel(c_ref, w_ref, b_ref, o_ref):
    c = c_ref[...]
    c_act = (c * jax.nn.sigmoid(c)).astype(BF16)
    acc = jnp.dot(c_act, w_ref[0].astype(BF16), preferred_element_type=F32)
    o_ref[0] = acc + b_ref[0]


def _ada(c8, w_ada, b_ada):
    L, D, N = w_ada.shape
    tn = _tile(N, 1024)
    return pl.pallas_call(
        _ada_kernel,
        grid=(L, N // tn),
        in_specs=[
            pl.BlockSpec((8, D), lambda l, j: (0, 0)),
            pl.BlockSpec((1, D, tn), lambda l, j: (l, 0, j)),
            pl.BlockSpec((1, 1, tn), lambda l, j: (l, 0, j)),
        ],
        out_specs=pl.BlockSpec((1, 8, tn), lambda l, j: (l, 0, j)),
        out_shape=jax.ShapeDtypeStruct((L, 8, N), F32),
        compiler_params=_cparams(("parallel", "parallel"), 56),
        name="ada",
    )(c8, w_ada, b_ada.reshape(L, 1, N))


def _norm_mod(x, g, scale, shift):
    xf = x
    y = xf * lax.rsqrt(jnp.mean(xf * xf, axis=-1, keepdims=True) + NORM_EPS)
    return (y * g) * (1.0 + scale) + shift


def _norm_mod_kernel(x_ref, g_ref, sc_ref, sh_ref, o_ref):
    o_ref[0] = _norm_mod(x_ref[0], g_ref[...], sc_ref[0], sh_ref[0]).astype(o_ref.dtype)


def _norm_mod_call(x, g, scale, shift):
    B, S, D = x.shape
    ts = _tile(S, 256)
    return pl.pallas_call(
        _norm_mod_kernel,
        grid=(B, S // ts),
        in_specs=[
            pl.BlockSpec((1, ts, D), lambda b, i: (b, i, 0)),
            pl.BlockSpec((1, D), lambda b, i: (0, 0)),
            pl.BlockSpec((1, 1, D), lambda b, i: (b, 0, 0)),
            pl.BlockSpec((1, 1, D), lambda b, i: (b, 0, 0)),
        ],
        out_specs=pl.BlockSpec((1, ts, D), lambda b, i: (b, i, 0)),
        out_shape=jax.ShapeDtypeStruct((B, S, D), BF16),
        compiler_params=_cparams(("parallel", "parallel")),
        name="norm_mod",
    )(x, g.reshape(1, D), scale, shift)


def _post_router_kernel(n_experts, x_ref, y_ref, gpost_ref, gate_ref, g_ref, sc_ref, sh_ref,
                        wr_hi_ref, wr_lo_ref, br_ref, x1_ref, h_ref, e_ref, p_ref):
    x1 = x_ref[0] + gate_ref[0] * _rms(y_ref[0].astype(F32), gpost_ref[...])
    x1_ref[0] = x1
    h = _norm_mod(x1, g_ref[...], sc_ref[0], sh_ref[0])
    h_ref[0] = _pack_bf16_pairs(h)
    h_hi = h.astype(BF16)
    h_lo = (h - h_hi.astype(F32)).astype(BF16)
    logits = (jnp.dot(h_hi, wr_hi_ref[...], preferred_element_type=F32)
              + jnp.dot(h_hi, wr_lo_ref[...], preferred_element_type=F32)
              + jnp.dot(h_lo, wr_hi_ref[...], preferred_element_type=F32)) + br_ref[...]
    lane = lax.broadcasted_iota(jnp.int32, logits.shape, 1)
    vals = jnp.where(lane < n_experts, logits, -jnp.inf)
    tops, idxs = [], []
    for _ in range(TOP_K):
        m = jnp.max(vals, axis=-1, keepdims=True)
        idx = jnp.min(jnp.where(vals == m, lane, LANES), axis=-1, keepdims=True)
        tops.append(m)
        idxs.append(idx)
        vals = jnp.where(lane == idx, -jnp.inf, vals)
    exps = [jnp.exp(t - tops[0]) for t in tops]
    denom = exps[0]
    for e in exps[1:]:
        denom = denom + e
    e_out = jnp.zeros(logits.shape, jnp.int32)
    p_out = jnp.zeros(logits.shape, F32)
    for k in range(TOP_K):
        e_out = jnp.where(lane == k, idxs[k], e_out)
        p_out = jnp.where(lane == k, exps[k] / denom, p_out)
    e_ref[...] = e_out
    p_ref[...] = p_out


def _post_router_call(x, y, g_post, gate, g_pre, scale, shift, w_router, b_router):
    B, S, D = x.shape
    E = w_router.shape[1]
    ts = _tile(S, 256)
    nb = S // ts
    wr = jnp.pad(w_router, ((0, 0), (0, LANES - E)))
    wr_hi = wr.astype(BF16)
    wr_lo = (wr - wr_hi.astype(F32)).astype(BF16)
    br = jnp.pad(b_router, (0, LANES - E)).reshape(1, LANES)
    blk = pl.BlockSpec((1, ts, D), lambda b, i: (b, i, 0))
    vec = pl.BlockSpec((1, D), lambda b, i: (0, 0))
    per_batch = pl.BlockSpec((1, 1, D), lambda b, i: (b, 0, 0))
    return pl.pallas_call(
        functools.partial(_post_router_kernel, E),
        grid=(B, nb),
        in_specs=[
            blk, blk, vec, per_batch, vec, per_batch, per_batch,
            pl.BlockSpec((D, LANES), lambda b, i: (0, 0)),
            pl.BlockSpec((D, LANES), lambda b, i: (0, 0)),
            pl.BlockSpec((1, LANES), lambda b, i: (0, 0)),
        ],
        out_specs=[
            blk,
            pl.BlockSpec((1, ts, D // 2), lambda b, i: (b, i, 0)),
            pl.BlockSpec((ts, LANES), lambda b, i: (b * nb + i, 0)),
            pl.BlockSpec((ts, LANES), lambda b, i: (b * nb + i, 0)),
        ],
        out_shape=[
            jax.ShapeDtypeStruct((B, S, D), F32),
            jax.ShapeDtypeStruct((B, S, D // 2), jnp.uint32),
            jax.ShapeDtypeStruct((B * S, LANES), jnp.int32),
            jax.ShapeDtypeStruct((B * S, LANES), F32),
        ],
        compiler_params=_cparams(("parallel", "parallel")),
        name="post_router",
    )(x, y.reshape(B, S, D), g_post.reshape(1, D), gate, g_pre.reshape(1, D), scale, shift,
      wr_hi, wr_lo, br)


def _mm_kernel(a_ref, w_ref, o_ref):
    o_ref[...] = jnp.dot(a_ref[...], w_ref[0], preferred_element_type=F32).astype(o_ref.dtype)


def _mm(a, w, layer, out_dtype, tm_pref=1024, tn_pref=512, name="mm"):
    M, K = a.shape
    N = w.shape[2]
    tm = _tile(M, tm_pref)
    tn = _tile(N, tn_pref)
    return pl.pallas_call(
        _mm_kernel,
        grid=(M // tm, N // tn),
        in_specs=[
            pl.BlockSpec((tm, K), lambda i, j: (i, 0)),
            pl.BlockSpec((1, K, tn), lambda i, j: (layer, 0, j)),
        ],
        out_specs=pl.BlockSpec((tm, tn), lambda i, j: (i, j)),
        out_shape=jax.ShapeDtypeStruct((M, N), out_dtype),
        compiler_params=_cparams(("parallel", "parallel"), 56),
        name=name,
    )(a, w)


def _mm_nt_kernel(a_ref, wt_ref, o_ref):
    o_ref[...] = lax.dot_general(a_ref[...], wt_ref[...], (((1,), (1,)), ((), ())),
                                 preferred_element_type=F32).astype(o_ref.dtype)


def _mm_nt(a, wt, out_dtype, tm_pref=1024, tn_pref=512, name="mm_nt"):
    M, K = a.shape
    N = wt.shape[0]
    tm = _tile(M, tm_pref)
    tn = _tile(N, tn_pref)
    return pl.pallas_call(
        _mm_nt_kernel,
        grid=(M // tm, N // tn),
        in_specs=[
            pl.BlockSpec((tm, K), lambda i, j: (i, 0)),
            pl.BlockSpec((tn, K), lambda i, j: (j, 0)),
        ],
        out_specs=pl.BlockSpec((tm, tn), lambda i, j: (i, j)),
        out_shape=jax.ShapeDtypeStruct((M, N), out_dtype),
        compiler_params=_cparams(("parallel", "parallel"), 56),
        name=name,
    )(a, wt)


def _conv_kernel(b_ref, c_ref, h_ref, w_ref, o_ref):
    g = c_ref[...].astype(F32) * h_ref[...].astype(F32)
    row = lax.broadcasted_iota(jnp.int32, g.shape, 0)
    g1 = jnp.where(row >= 1, pltpu.roll(g, 1, 0), 0.0)
    g2 = jnp.where(row >= 2, pltpu.roll(g, 2, 0), 0.0)
    w = w_ref[...]
    conv = w[0:1] * g2 + w[1:2] * g1 + w[2:3] * g
    o_ref[...] = (b_ref[...].astype(F32) * conv).astype(o_ref.dtype)


def _conv_call(u, conv_w, B, S, C):
    tc = _tile(C, 256)
    nc = C // tc
    w8 = jnp.pad(conv_w, ((0, 8 - CONV_K), (0, 0)))
    return pl.pallas_call(
        _conv_kernel,
        grid=(B, nc),
        in_specs=[
            pl.BlockSpec((S, tc), lambda b, j: (b, j)),
            pl.BlockSpec((S, tc), lambda b, j: (b, nc + j)),
            pl.BlockSpec((S, tc), lambda b, j: (b, 2 * nc + j)),
            pl.BlockSpec((8, tc), lambda b, j: (0, j)),
        ],
        out_specs=pl.BlockSpec((S, tc), lambda b, j: (b, j)),
        out_shape=jax.ShapeDtypeStruct((B * S, C), BF16),
        compiler_params=_cparams(("parallel", "parallel")),
        name="gated_conv",
    )(u, u, u, w8)


def _rms(x, g):
    return x * lax.rsqrt(jnp.mean(x * x, axis=-1, keepdims=True) + NORM_EPS) * g


def _mla_q_kernel(n_heads, h_ref, wc_ref, g_ref, wa_ref, wb_ref, cos_ref, sin_ref, q_ref):
    c_q = jnp.dot(h_ref[...], wc_ref[...], preferred_element_type=F32)
    n = _rms(c_q, g_ref[...]).astype(BF16)
    a = jnp.dot(n, wa_ref[...], preferred_element_type=F32)
    bm = jnp.dot(n, wb_ref[...], preferred_element_type=F32)
    cos = cos_ref[...]
    sin = sin_ref[...]
    for hd in range(n_heads):
        o = hd * QK_PAD
        q_ref[:, o:o + LANES] = a[:, o:o + LANES].astype(q_ref.dtype)
        rot = a[:, o + LANES:o + QK_PAD] * cos + bm[:, hd * LANES:(hd + 1) * LANES] * sin
        q_ref[:, o + LANES:o + QK_PAD] = rot.astype(q_ref.dtype)


def _mla_q_call(h, w_cq, q_norm_g, w_a, w_b, cos_t, sin_t, n_heads):
    M, D = h.shape
    R = w_cq.shape[1]
    tm = _tile(M, 512)
    const = lambda i: (0, 0)
    return pl.pallas_call(
        functools.partial(_mla_q_kernel, n_heads),
        grid=(M // tm,),
        in_specs=[
            pl.BlockSpec((tm, D), lambda i: (i, 0)),
            pl.BlockSpec((D, R), const),
            pl.BlockSpec((1, R), const),
            pl.BlockSpec((R, n_heads * QK_PAD), const),
            pl.BlockSpec((R, n_heads * LANES), const),
            pl.BlockSpec((tm, LANES), lambda i: (i, 0)),
            pl.BlockSpec((tm, LANES), lambda i: (i, 0)),
        ],
        out_specs=pl.BlockSpec((tm, n_heads * QK_PAD), lambda i: (i, 0)),
        out_shape=jax.ShapeDtypeStruct((M, n_heads * QK_PAD), BF16),
        compiler_params=_cparams(("parallel",), 56),
        name="mla_q",
    )(h, w_cq, q_norm_g.reshape(1, R), w_a, w_b, cos_t, sin_t)


def _mla_kv_kernel(n_heads, h_ref, wc_ref, g_ref, wk_ref, wv_ref, wra_ref, wrb_ref,
                   cos_ref, sin_ref, k_ref, v_ref):
    hb = h_ref[...]
    c_kv = jnp.dot(hb, wc_ref[...], preferred_element_type=F32)
    n = _rms(c_kv, g_ref[...]).astype(BF16)
    k_nope = jnp.dot(n, wk_ref[...], preferred_element_type=F32)
    v_ref[...] = jnp.dot(n, wv_ref[...], preferred_element_type=F32).astype(v_ref.dtype)
    ra = jnp.dot(hb, wra_ref[...], preferred_element_type=F32)
    rb = jnp.dot(hb, wrb_ref[...], preferred_element_type=F32)
    k_rope = (ra * cos_ref[...] + rb * sin_ref[...]).astype(k_ref.dtype)
    for hd in range(n_heads):
        o = hd * QK_PAD
        k_ref[:, o:o + LANES] = k_nope[:, hd * LANES:(hd + 1) * LANES].astype(k_ref.dtype)
        k_ref[:, o + LANES:o + QK_PAD] = k_rope


def _mla_kv_call(h, w_ckv, kv_norm_g, w_k, w_v, w_ra, w_rb, cos_t, sin_t, n_heads):
    M, D = h.shape
    R = w_ckv.shape[1]
    tm = _tile(M, 512)
    const = lambda i: (0, 0)
    return pl.pallas_call(
        functools.partial(_mla_kv_kernel, n_heads),
        grid=(M // tm,),
        in_specs=[
            pl.BlockSpec((tm, D), lambda i: (i, 0)),
            pl.BlockSpec((D, R), const),
            pl.BlockSpec((1, R), const),
            pl.BlockSpec((R, n_heads * LANES), const),
            pl.BlockSpec((R, n_heads * LANES), const),
            pl.BlockSpec((D, LANES), const),
            pl.BlockSpec((D, LANES), const),
            pl.BlockSpec((tm, LANES), lambda i: (i, 0)),
            pl.BlockSpec((tm, LANES), lambda i: (i, 0)),
        ],
        out_specs=[
            pl.BlockSpec((tm, n_heads * QK_PAD), lambda i: (i, 0)),
            pl.BlockSpec((tm, n_heads * LANES), lambda i: (i, 0)),
        ],
        out_shape=[
            jax.ShapeDtypeStruct((M, n_heads * QK_PAD), BF16),
            jax.ShapeDtypeStruct((M, n_heads * LANES), BF16),
        ],
        compiler_params=_cparams(("parallel",), 56),
        name="mla_kv",
    )(h, w_ckv, kv_norm_g.reshape(1, R), w_k, w_v, w_ra, w_rb, cos_t, sin_t)


def _fox_prep_kernel(n_heads, inv_scale, h_ref, wf_ref, bf_ref, q_ref, k_ref,
                     qo_ref, ko_ref, carry_ref):
    @pl.when(pl.program_id(1) == 0)
    def _():
        carry_ref[...] = jnp.zeros_like(carry_ref)

    f = jnp.dot(h_ref[...], wf_ref[...], preferred_element_type=F32) + bf_ref[...]
    log_f = jax.nn.log_sigmoid(f)
    ts = log_f.shape[0]
    row = lax.broadcasted_iota(jnp.int32, (ts, ts), 0)
    col = lax.broadcasted_iota(jnp.int32, (ts, ts), 1)
    tri = jnp.where(col <= row, 1.0, 0.0).astype(BF16)
    cum = carry_ref[0:1, :]
    for part in _split3(log_f):
        cum = cum + jnp.dot(tri, part, preferred_element_type=F32)
    carry_ref[...] = jnp.broadcast_to(cum[ts - 1:ts, :], carry_ref.shape)

    lane = lax.broadcasted_iota(jnp.int32, (ts, LANES), 1)
    cum_s = cum * inv_scale
    for hd in range(n_heads):
        colv = jnp.broadcast_to(cum_s[:, hd:hd + 1], (ts, LANES))
        hi, mid, lo = (part.astype(F32) for part in _split3(colv))
        q_aug = jnp.where(lane == 0, hi, jnp.where(lane == 1, mid, jnp.where(
            lane == 2, lo, jnp.where(lane < 6, 1.0, 0.0))))
        k_aug = jnp.where(lane < 3, 1.0, jnp.where(lane == 3, -hi, jnp.where(
            lane == 4, -mid, jnp.where(lane == 5, -lo, 0.0))))
        o = hd * QK_PAD
        qo_ref[:, o:o + LANES] = q_ref[:, hd * LANES:(hd + 1) * LANES]
        qo_ref[:, o + LANES:o + QK_PAD] = q_aug.astype(BF16)
        ko_ref[:, o:o + LANES] = k_ref[:, hd * LANES:(hd + 1) * LANES]
        ko_ref[:, o + LANES:o + QK_PAD] = k_aug.astype(BF16)


def _fox_prep_call(h, u, w_f, b_f, B, S, n_heads, q_blk, k_blk, scale):
    M, D = h.shape
    W = n_heads * LANES
    ts = _tile(S, 512)
    nb = S // ts
    row = lambda b, i: (b * nb + i, 0)
    return pl.pallas_call(
        functools.partial(_fox_prep_kernel, n_heads, 1.0 / scale),
        grid=(B, nb),
        in_specs=[
            pl.BlockSpec((ts, D), row),
            pl.BlockSpec((D, LANES), lambda b, i: (0, 0)),
            pl.BlockSpec((1, LANES), lambda b, i: (0, 0)),
            pl.BlockSpec((ts, W), lambda b, i: (b * nb + i, q_blk)),
            pl.BlockSpec((ts, W), lambda b, i: (b * nb + i, k_blk)),
        ],
        out_specs=[
            pl.BlockSpec((ts, n_heads * QK_PAD), row),
            pl.BlockSpec((ts, n_heads * QK_PAD), row),
        ],
        out_shape=[
            jax.ShapeDtypeStruct((M, n_heads * QK_PAD), BF16),
            jax.ShapeDtypeStruct((M, n_heads * QK_PAD), BF16),
        ],
        scratch_shapes=[pltpu.VMEM((8, LANES), F32)],
        compiler_params=_cparams(("parallel", "arbitrary")),
        name="fox_prep",
    )(h, w_f, b_f, u, u)


def _flash_kernel(scale_log2, t, hp, q_ref, k_ref, v_ref, o_ref, m_ref, l_ref, acc_ref):
    i = pl.program_id(2)
    m_ref[...] = jnp.full_like(m_ref, -jnp.inf)
    l_ref[...] = jnp.zeros_like(l_ref)
    acc_ref[...] = jnp.zeros_like(acc_ref)

    def tile(j, masked):
        off = pl.multiple_of(j * t, t)
        for hh in range(hp):
            q = q_ref[:, hh * QK_PAD:(hh + 1) * QK_PAD]
            k = k_ref[pl.ds(off, t), hh * QK_PAD:(hh + 1) * QK_PAD]
            v = v_ref[pl.ds(off, t), hh * HEAD_DIM:(hh + 1) * HEAD_DIM]
            s = lax.dot_general(q, k, (((1,), (1,)), ((), ())),
                                preferred_element_type=F32) * scale_log2
            if masked:
                row = lax.broadcasted_iota(jnp.int32, s.shape, 0)
                col = lax.broadcasted_iota(jnp.int32, s.shape, 1)
                s = jnp.where(col <= row, s, -jnp.inf)
            parts = [s[:, c * LANES:(c + 1) * LANES] for c in range(t // LANES)]
            part_max = parts[0]
            for part in parts[1:]:
                part_max = jnp.maximum(part_max, part)
            m_prev = m_ref[hh]
            m_new = jnp.maximum(m_prev, jnp.max(part_max, axis=-1, keepdims=True))
            alpha = jnp.exp2(m_prev - m_new)
            probs = [jnp.exp2(part - m_new) for part in parts]
            l_part = probs[0]
            for pr in probs[1:]:
                l_part = l_part + pr
            l_ref[hh] = alpha * l_ref[hh] + l_part
            p = jnp.concatenate(probs, axis=1).astype(v.dtype)
            acc_ref[hh] = alpha * acc_ref[hh] + jnp.dot(p, v, preferred_element_type=F32)
            m_ref[hh] = m_new

    def body(j, c):
        tile(j, False)
        return c

    lax.fori_loop(0, i, body, 0)
    tile(i, True)
    for hh in range(hp):
        l_row = jnp.sum(l_ref[hh], axis=-1, keepdims=True)
        o_ref[:, hh * HEAD_DIM:(hh + 1) * HEAD_DIM] = (acc_ref[hh] / l_row).astype(o_ref.dtype)


def _flash_call(q, k, v, B, S, n_heads, scale, v_blk0=0, name="flash"):
    M = q.shape[0]
    t = _tile(S, 1024)
    nq = S // t
    hp = 2
    assert n_heads % hp == 0 and v_blk0 % hp == 0
    return pl.pallas_call(
        functools.partial(_flash_kernel, scale * 1.4426950408889634, t, hp),
        grid=(B, n_heads // hp, nq),
        in_specs=[
            pl.BlockSpec((t, hp * QK_PAD), lambda b, h, i: (b * nq + i, h)),
            pl.BlockSpec((S, hp * QK_PAD), lambda b, h, i: (b, h)),
            pl.BlockSpec((S, hp * HEAD_DIM), lambda b, h, i: (b, v_blk0 // hp + h)),
        ],
        out_specs=pl.BlockSpec((t, hp * HEAD_DIM), lambda b, h, i: (b * nq + i, h)),
        out_shape=jax.ShapeDtypeStruct((M, n_heads * HEAD_DIM), BF16),
        scratch_shapes=[
            pltpu.VMEM((hp, t, LANES), F32),
            pltpu.VMEM((hp, t, LANES), F32),
            pltpu.VMEM((hp, t, HEAD_DIM), F32),
        ],
        compiler_params=_cparams(("parallel", "parallel", "arbitrary")),
        name=name,
    )(q, k, v)


def _merge_kernel(n_br, h_ref, wg_ref, bg_ref, *rest):
    br_refs = rest[:n_br]
    wb_ref = rest[n_br]
    o_ref = rest[n_br + 1]
    hb = h_ref[...]
    merged = None
    for n in range(n_br):
        gate = jax.nn.sigmoid(jnp.dot(hb, wg_ref[0, n], preferred_element_type=F32) + bg_ref[n])
        proj = jnp.dot(br_refs[n][...], wb_ref[0, n], preferred_element_type=F32)
        merged = gate * proj if merged is None else merged + gate * proj
    o_ref[...] = merged.astype(o_ref.dtype)


def _merge_call(h, w_gate, layer, b_gate, branches, w_branch):
    M, D = h.shape
    _, n_br, W, _ = w_branch.shape
    tm = _tile(M, 1024)
    tn = _tile(D, 256)
    in_specs = [
        pl.BlockSpec((tm, D), lambda i, j: (i, 0)),
        pl.BlockSpec((1, n_br, D, tn), lambda i, j: (layer, 0, 0, j)),
        pl.BlockSpec((n_br, 1, tn), lambda i, j: (0, 0, j)),
    ]
    in_specs += [pl.BlockSpec((tm, W), lambda i, j: (i, 0)) for _ in range(n_br)]
    in_specs += [pl.BlockSpec((1, n_br, W, tn), lambda i, j: (layer, 0, 0, j))]
    return pl.pallas_call(
        functools.partial(_merge_kernel, n_br),
        grid=(M // tm, D // tn),
        in_specs=in_specs,
        out_specs=pl.BlockSpec((tm, tn), lambda i, j: (i, j)),
        out_shape=jax.ShapeDtypeStruct((M, D), BF16),
        compiler_params=_cparams(("parallel", "parallel"), 56),
        name="gated_merge",
    )(h, w_gate, b_gate.reshape(n_br, 1, D), *branches, w_branch)


def _combine_post_kernel(ts, n_steps, dest_ref, x_ref, y_hbm, p_ref, g_ref, gate_ref, o_ref,
                         ybuf, sems):
    s = pl.program_id(0)

    def issue(step, slot):
        def tok_body(r, c):
            t = step * ts + r
            for k in range(TOP_K):
                pltpu.make_async_copy(y_hbm.at[pl.ds(dest_ref[t * TOP_K + k], 1)],
                                      ybuf.at[slot, k, pl.ds(r, 1)], sems.at[slot]).start()
            return c
        lax.fori_loop(0, ts, tok_body, 0)

    @pl.when(s == 0)
    def _():
        issue(0, 0)

    @pl.when(s + 1 < n_steps)
    def _():
        issue(s + 1, (s + 1) % 2)

    slot = s % 2
    pltpu.make_async_copy(ybuf.at[slot], ybuf.at[slot], sems.at[slot]).wait()
    p = p_ref[...]
    y = _unpack_bf16_pairs(ybuf[slot, 0]) * p[:, 0:1]
    for k in range(1, TOP_K):
        y = y + _unpack_bf16_pairs(ybuf[slot, k]) * p[:, k:k + 1]
    o_ref[0] = x_ref[0] + gate_ref[0] * _rms(y, g_ref[...])


def _combine_post_call(x, y_grouped, dest_flat, top_p, g, gate):
    B, S, D = x.shape
    ts = _tile(S, 128)
    nb = S // ts
    blk = pl.BlockSpec((1, ts, D), lambda s, d: (s // nb, s % nb, 0))
    grid_spec = pltpu.PrefetchScalarGridSpec(
        num_scalar_prefetch=1,
        grid=(B * nb,),
        in_specs=[blk,
                  pl.BlockSpec(memory_space=pl.ANY),
                  pl.BlockSpec((ts, LANES), lambda s, d: (s, 0)),
                  pl.BlockSpec((1, D), lambda s, d: (0, 0)),
                  pl.BlockSpec((1, 1, D), lambda s, d: (s // nb, 0, 0))],
        out_specs=blk,
        scratch_shapes=[pltpu.VMEM((2, TOP_K, ts, D // 2), jnp.uint32),
                        pltpu.SemaphoreType.DMA((2,))],
    )
    return pl.pallas_call(
        functools.partial(_combine_post_kernel, ts, B * nb),
        grid_spec=grid_spec,
        out_shape=jax.ShapeDtypeStruct((B, S, D), F32),
        compiler_params=_cparams(("arbitrary",)),
        name="combine_post",
    )(dest_flat, x, y_grouped, top_p, g.reshape(1, D), gate)


def _rank_kernel(e_ref, rank_ref, cnt_ref, carry_ref):
    @pl.when(pl.program_id(0) == 0)
    def _():
        carry_ref[...] = jnp.zeros_like(carry_ref)

    e = e_ref[...]
    tb = e.shape[0]
    lane = lax.broadcasted_iota(jnp.int32, e.shape, 1)
    sel = [jnp.broadcast_to(e[:, k:k + 1], e.shape) == lane for k in range(TOP_K)]
    onehot = jnp.zeros(e.shape, F32)
    for s in sel:
        onehot = onehot + jnp.where(s, 1.0, 0.0)
    row = lax.broadcasted_iota(jnp.int32, (tb, tb), 0)
    col = lax.broadcasted_iota(jnp.int32, (tb, tb), 1)
    strict = jnp.where(col < row, 1.0, 0.0).astype(BF16)
    before = carry_ref[0:1, :] + jnp.dot(strict, onehot.astype(BF16), preferred_element_type=F32)
    rank = jnp.zeros(e.shape, F32)
    for k in range(TOP_K):
        r_k = jnp.sum(jnp.where(sel[k], before, 0.0), axis=-1, keepdims=True)
        rank = jnp.where(lane == k, r_k, rank)
    rank_ref[...] = rank.astype(jnp.int32)
    total = carry_ref[0:1, :] + jnp.sum(onehot, axis=0, keepdims=True)
    carry_ref[...] = jnp.broadcast_to(total, carry_ref.shape)
    cnt_ref[...] = jnp.broadcast_to(total, cnt_ref.shape).astype(jnp.int32)


def _rank_call(top_e):
    N = top_e.shape[0]
    tb = _tile(N, 512)
    return pl.pallas_call(
        _rank_kernel,
        grid=(N // tb,),
        in_specs=[pl.BlockSpec((tb, LANES), lambda i: (i, 0))],
        out_specs=[pl.BlockSpec((tb, LANES), lambda i: (i, 0)),
                   pl.BlockSpec((8, LANES), lambda i: (0, 0))],
        out_shape=[jax.ShapeDtypeStruct((N, LANES), jnp.int32),
                   jax.ShapeDtypeStruct((8, LANES), jnp.int32)],
        scratch_shapes=[pltpu.VMEM((8, LANES), F32)],
        compiler_params=_cparams(("arbitrary",)),
        name="route_rank",
    )(top_e)


def _row_copy(src, dst, s_row, d_row, sem):
    return pltpu.make_async_copy(src.at[pl.ds(s_row, 1)], dst.at[pl.ds(d_row, 1)], sem)


def _rows_wait(ref, n, sem):
    pltpu.make_async_copy(ref.at[pl.ds(0, n)], ref.at[pl.ds(0, n)], sem).wait()


def _dispatch_kernel(tb, n_experts, n_blocks, dest_ref, zlo_ref, zhi_ref, nblk_ref,
                     h_ref, xg_ref, zero_ref, sem):
    i = pl.program_id(0)

    @pl.when(i == 0)
    def _():
        zero_ref[...] = jnp.zeros_like(zero_ref)

        def exp_body(e, carry):
            def issue(r, c2):
                _row_copy(zero_ref, xg_ref, 0, r, sem).start()
                return c2

            def wait(r, c2):
                _row_copy(zero_ref, xg_ref, 0, r, sem).wait()
                return c2

            lax.fori_loop(zlo_ref[e], zhi_ref[e], issue, 0)
            lax.fori_loop(zlo_ref[e], zhi_ref[e], wait, 0)
            return carry

        lax.fori_loop(0, n_experts, exp_body, 0)

        def tail_body(b, carry):
            cp = pltpu.make_async_copy(
                zero_ref,
                xg_ref.at[pl.ds(pl.multiple_of(b * EXPERT_ROWS, EXPERT_ROWS), EXPERT_ROWS)], sem)
            cp.start()
            cp.wait()
            return carry

        lax.fori_loop(nblk_ref[0], n_blocks, tail_body, 0)

    def tok_body(r, carry):
        t = i * tb + r
        for k in range(TOP_K):
            _row_copy(h_ref, xg_ref, r, dest_ref[t * TOP_K + k], sem).start()
        return carry

    lax.fori_loop(0, tb, tok_body, 0)
    _rows_wait(xg_ref, tb * TOP_K, sem)


def _dispatch_call(h, dest_flat, zlo, zhi, n_used, n_rows):
    N, W = h.shape
    E = zlo.shape[0]
    tb = _tile(N, 256)
    grid_spec = pltpu.PrefetchScalarGridSpec(
        num_scalar_prefetch=4,
        grid=(N // tb,),
        in_specs=[pl.BlockSpec((tb, W), lambda i, *_: (i, 0))],
        out_specs=pl.BlockSpec(memory_space=pl.ANY),
        scratch_shapes=[pltpu.VMEM((EXPERT_ROWS, W), h.dtype), pltpu.SemaphoreType.DMA(())],
    )
    return pl.pallas_call(
        functools.partial(_dispatch_kernel, tb, E, n_rows // EXPERT_ROWS),
        grid_spec=grid_spec,
        out_shape=jax.ShapeDtypeStruct((n_rows, W), h.dtype),
        compiler_params=pltpu.CompilerParams(dimension_semantics=("arbitrary",),
                                             has_side_effects=True),
        name="moe_dispatch",
    )(dest_flat, zlo, zhi, n_used, h)


def _expert_kernel(f_dim, be_ref, nu_ref, x_ref, wgu_ref, bgu_ref, wd_ref, bd_ref, o_ref):
    i = pl.program_id(0)

    @pl.when(i < nu_ref[0])
    def _():
        xb = _unpack_bf16_pairs(x_ref[...]).astype(BF16)
        gu = jnp.dot(xb, wgu_ref[0, 0], preferred_element_type=F32) + bgu_ref[0]
        g = jnp.minimum(gu[:, :f_dim], SWIGLU_LIMIT)
        up = jnp.clip(gu[:, f_dim:], -SWIGLU_LIMIT, SWIGLU_LIMIT)
        act = g * jax.nn.sigmoid(SWIGLU_ALPHA * g) * (up + 1.0)
        y = jnp.dot(act.astype(BF16), wd_ref[0, 0], preferred_element_type=F32) + bd_ref[0]
        o_ref[...] = _pack_bf16_pairs(y)

    @pl.when(i >= nu_ref[0])
    def _():
        o_ref[...] = jnp.zeros_like(o_ref)


def _expert_call(x_grouped, block_e, n_used, w_gu, layer, b_gu, w_d, b_d):
    _, E, D, F2 = w_gu.shape
    n_rows = x_grouped.shape[0]
    F = F2 // 2
    n_blocks = n_rows // EXPERT_ROWS
    grid_spec = pltpu.PrefetchScalarGridSpec(
        num_scalar_prefetch=2,
        grid=(n_blocks,),
        in_specs=[
            pl.BlockSpec((EXPERT_ROWS, D // 2), lambda i, be, nu: (jnp.minimum(i, nu[0] - 1), 0)),
            pl.BlockSpec((1, 1, D, F2), lambda i, be, nu: (layer, be[i], 0, 0)),
            pl.BlockSpec((1, 1, F2), lambda i, be, nu: (be[i], 0, 0)),
            pl.BlockSpec((1, 1, F, D), lambda i, be, nu: (layer, be[i], 0, 0)),
            pl.BlockSpec((1, 1, D), lambda i, be, nu: (be[i], 0, 0)),
        ],
        out_specs=pl.BlockSpec((EXPERT_ROWS, D // 2), lambda i, be, nu: (i, 0)),
    )
    return pl.pallas_call(
        functools.partial(_expert_kernel, F),
        grid_spec=grid_spec,
        out_shape=jax.ShapeDtypeStruct((n_rows, D // 2), jnp.uint32),
        compiler_params=_cparams(("arbitrary",), 56),
        name="experts",
    )(block_e, n_used, x_grouped, w_gu, b_gu.reshape(E, 1, F2), w_d, b_d.reshape(E, 1, D))


def _mixer_weights(D, w_mix_t, w_uq, w_ukv, fox_f_bias):
    Wb = D // 4
    H = Wb // HEAD_DIM
    q_lora = 3 * D // 16
    kv_lora = D // 8
    half = ROPE_DIM // 2
    o_cq = 3 * Wb
    o_ckv = o_cq + q_lora
    o_kr = o_ckv + kv_lora
    o_fox = o_kr + ROPE_DIM
    o_f = o_fox + 3 * Wb
    w_convfox_t = jnp.concatenate([w_mix_t[:o_cq], w_mix_t[o_fox:o_f]], axis=0).astype(BF16)
    latent = w_mix_t[o_cq:o_fox].astype(BF16).T
    w_cq = latent[:, :q_lora]
    w_ckv = latent[:, q_lora:q_lora + kv_lora]
    w_kr = latent[:, q_lora + kv_lora:]
    zpad = jnp.zeros((D, LANES - ROPE_DIM), BF16)
    w_kra = jnp.concatenate([w_kr, zpad], axis=1)
    w_krb = jnp.concatenate([-w_kr[:, half:], w_kr[:, :half], zpad], axis=1)
    w_f = jnp.pad(w_mix_t[o_f:o_f + H].astype(BF16).T, ((0, 0), (0, LANES - H)))
    b_f = jnp.pad(fox_f_bias, (0, LANES - H)).reshape(1, LANES)

    uq = w_uq.reshape(q_lora, H, HEAD_DIM + ROPE_DIM)
    zq = jnp.zeros((q_lora, H, QK_PAD - HEAD_DIM - ROPE_DIM), BF16)
    w_qa = jnp.concatenate([uq, zq], axis=2).reshape(q_lora, H * QK_PAD)
    r1 = uq[:, :, HEAD_DIM:HEAD_DIM + half]
    r2 = uq[:, :, HEAD_DIM + half:]
    w_qb = jnp.concatenate([-r2, r1, zq], axis=2).reshape(q_lora, H * LANES)

    ukv = w_ukv.reshape(kv_lora, H, 2 * HEAD_DIM)
    w_k = ukv[:, :, :HEAD_DIM].reshape(kv_lora, H * HEAD_DIM)
    w_v = ukv[:, :, HEAD_DIM:].reshape(kv_lora, H * HEAD_DIM)
    return dict(w_convfox_t=w_convfox_t, w_cq=w_cq, w_ckv=w_ckv, w_kra=w_kra, w_krb=w_krb,
                w_f=w_f, b_f=b_f, w_qa=w_qa, w_qb=w_qb, w_k=w_k, w_v=w_v)


def _mixer(h, B, S, D, rope_cos, rope_sin, mw, conv_w, q_norm_g, kv_norm_g,
           w_branch, w_merge_gate, layer, b_merge_gate, w_o):
    Wb = D // 4
    H = Wb // HEAD_DIM
    u = _mm_nt(h, mw["w_convfox_t"], BF16, name="mix_in")
    o_a = _conv_call(u, conv_w, B, S, Wb)

    q_mla = _mla_q_call(h, mw["w_cq"], q_norm_g, mw["w_qa"], mw["w_qb"], rope_cos, rope_sin, H)
    k_mla, v_mla = _mla_kv_call(h, mw["w_ckv"], kv_norm_g, mw["w_k"], mw["w_v"],
                                mw["w_kra"], mw["w_krb"], rope_cos, rope_sin, H)
    o_b = _flash_call(q_mla, k_mla, v_mla, B, S, H, (HEAD_DIM + ROPE_DIM) ** -0.5, name="flash_mla")

    fox_scale = HEAD_DIM ** -0.5
    q_fox, k_fox = _fox_prep_call(h, u, mw["w_f"], mw["b_f"], B, S, H, 3, 4, fox_scale)
    o_c = _flash_call(q_fox, k_fox, u, B, S, H, fox_scale, v_blk0=5 * H, name="flash_fox")

    merged = _merge_call(h, w_merge_gate, layer, b_merge_gate, (o_a, o_b, o_c), w_branch)
    return _mm(merged, w_o, layer, BF16, name="mix_out")


def _moe(x, h, top_e, top_p, n_experts, g_post, gate, w_gate_up, layer, b_gate_up, w_down, b_down):
    B, S, D = x.shape
    N = B * S
    E = n_experts
    rank, counts8 = _rank_call(top_e)

    counts = counts8[0, :E]
    padded = (counts + EXPERT_ROWS - 1) // EXPERT_ROWS * EXPERT_ROWS
    pad_end = jnp.cumsum(padded)
    pad_start = pad_end - padded
    e_flat = top_e[:, :TOP_K].reshape(-1)
    dest = pad_start[e_flat] + rank[:, :TOP_K].reshape(-1)
    n_blocks = (N * TOP_K + E * (EXPERT_ROWS - 1) + EXPERT_ROWS - 1) // EXPERT_ROWS
    n_rows = n_blocks * EXPERT_ROWS
    block_start = jnp.arange(n_blocks, dtype=jnp.int32) * EXPERT_ROWS
    block_e = jnp.minimum(
        jnp.sum((pad_end[None, :] <= block_start[:, None]).astype(jnp.int32), axis=1), E - 1)
    n_used = (pad_end[-1:] // EXPERT_ROWS).astype(jnp.int32)

    x_grouped = _dispatch_call(h.reshape(N, D // 2), dest.astype(jnp.int32),
                               (pad_start + counts).astype(jnp.int32),
                               pad_end.astype(jnp.int32), n_used, n_rows)
    y_grouped = _expert_call(x_grouped, block_e, n_used, w_gate_up, layer, b_gate_up,
                             w_down, b_down)
    return _combine_post_call(x, y_grouped, dest.astype(jnp.int32), top_p, g_post, gate)


def kernel(x, c, positions, w_ada, b_ada, g_mix_pre, g_mix_post, g_ffn_pre, g_ffn_post, w_mix_in, conv_w, q_norm_g, kv_norm_g, w_uq, w_ukv, fox_f_bias, w_branch, w_merge_gate, b_merge_gate, w_o, w_router, b_router, w_gate_up, b_gate_up, w_down, b_down):
    B, S, D = x.shape
    L = w_ada.shape[0]
    N = B * S

    c8 = jnp.pad(c, ((0, 8 - B), (0, 0)))
    ada = _ada(c8, w_ada, b_ada)[:, :B].reshape(L, B, N_ADA, 1, D)

    half = ROPE_DIM // 2
    inv_freq = ROPE_THETA ** (-jnp.arange(half, dtype=F32) / half)
    ang = positions.astype(F32).reshape(N, 1) * inv_freq
    zer = jnp.zeros((N, LANES - ROPE_DIM), F32)
    rope_cos = jnp.concatenate([jnp.cos(ang), jnp.cos(ang), zer], axis=1)
    rope_sin = jnp.concatenate([jnp.sin(ang), jnp.sin(ang), zer], axis=1)

    w_uq_bf, w_ukv_bf = w_uq.astype(BF16), w_ukv.astype(BF16)
    w_branch_bf, w_gate_bf, w_o_bf = (w.astype(BF16) for w in (w_branch, w_merge_gate, w_o))
    w_gu_bf, w_down_bf = w_gate_up.astype(BF16), w_down.astype(BF16)
    w_mix_t = jnp.swapaxes(w_mix_in, 1, 2)

    for l in range(L):
        shift_m, scale_m, gate_m = ada[l, :, 0], ada[l, :, 1], ada[l, :, 2]
        shift_f, scale_f, gate_f = ada[l, :, 3], ada[l, :, 4], ada[l, :, 5]

        h = _norm_mod_call(x, g_mix_pre[l], scale_m, shift_m).reshape(N, D)
        mw = _mixer_weights(D, w_mix_t[l], w_uq_bf[l], w_ukv_bf[l], fox_f_bias[l])
        y = _mixer(h, B, S, D, rope_cos, rope_sin, mw, conv_w[l], q_norm_g[l], kv_norm_g[l],
                   w_branch_bf, w_gate_bf, l, b_merge_gate[l], w_o_bf)
        x, h_ffn, top_e, top_p = _post_router_call(x, y, g_mix_post[l], gate_m, g_ffn_pre[l],
                                                   scale_f, shift_f, w_router[l], b_router[l])
        x = _moe(x, h_ffn, top_e, top_p, w_router.shape[2], g_ffn_post[l], gate_f,
                 w_gu_bf, l, b_gate_up[l], w_down_bf, b_down[l])
    return x
```

```python
import functools

import jax
import jax.numpy as jnp
from jax import lax
from jax.experimental import pallas as pl
from jax.experimental.pallas import tpu as pltpu

NORM_EPS = 1e-6
ROPE_THETA = 10000.0
HEAD_DIM = 128
ROPE_DIM = 64
CONV_K = 3
TOP_K = 4
SWIGLU_ALPHA = 1.702
SWIGLU_LIMIT = 7.0
N_ADA = 6
LANES = 128
QK_PAD = 2 * LANES
EXPERT_ROWS = 256
MIB = 1024 * 1024

F32 = jnp.float32
BF16 = jnp.bfloat16


def _cparams(sem, vmem_mib=48):
    return pltpu.CompilerParams(dimension_semantics=sem, vmem_limit_bytes=vmem_mib * MIB)


def _tile(n, pref):
    t = min(pref, n)
    while n % t:
        t //= 2
    return t


def _pack_bf16_pairs(x):
    n = x.shape[1] // 2
    lo = lax.bitcast_convert_type(x[:, :n].astype(BF16).astype(F32), jnp.uint32)
    hi = lax.bitcast_convert_type(x[:, n:].astype(BF16).astype(F32), jnp.uint32)
    return hi | (lo >> 16)


def _unpack_bf16_pairs(w):
    lo = lax.bitcast_convert_type(w << 16, F32)
    hi = lax.bitcast_convert_type(w & jnp.uint32(0xFFFF0000), F32)
    return jnp.concatenate([lo, hi], axis=1)


def _split3(x):
    hi = x.astype(BF16)
    r1 = x - hi.astype(F32)
    mid = r1.astype(BF16)
    lo = (r1 - mid.astype(F32)).astype(BF16)
    return hi, mid, lo


def _ada_kernel(c_ref, w_ref, b_ref, o_ref):
    c = c_ref[...]
    c_act = (c * jax.nn.sigmoid(c)).astype(BF16)
    acc = jnp.dot(c_act, w_ref[0].astype(BF16), preferred_element_type=F32)
    o_ref[0] = acc + b_ref[0]


def _ada(c8, w_ada, b_ada):
    L, D, N = w_ada.shape
    tn = _tile(N, 1024)
    return pl.pallas_call(
        _ada_kernel,
        grid=(L, N // tn),
        in_specs=[
            pl.BlockSpec((8, D), lambda l, j: (0, 0)),
            pl.BlockSpec((1, D, tn), lambda l, j: (l, 0, j)),
            pl.BlockSpec((1, 1, tn), lambda l, j: (l, 0, j)),
        ],
        out_specs=pl.BlockSpec((1, 8, tn), lambda l, j: (l, 0, j)),
        out_shape=jax.ShapeDtypeStruct((L, 8, N), F32),
        compiler_params=_cparams(("parallel", "parallel"), 56),
        name="ada",
    )(c8, w_ada, b_ada.reshape(L, 1, N))


def _norm_mod(x, g, scale, shift):
    xf = x
    y = xf * lax.rsqrt(jnp.mean(xf * xf, axis=-1, keepdims=True) + NORM_EPS)
    return (y * g) * (1.0 + scale) + shift


def _norm_mod_kernel(x_ref, g_ref, sc_ref, sh_ref, o_ref):
    o_ref[0] = _norm_mod(x_ref[0], g_ref[...], sc_ref[0], sh_ref[0]).astype(o_ref.dtype)


def _norm_mod_call(x, g, scale, shift):
    B, S, D = x.shape
    ts = _tile(S, 256)
    return pl.pallas_call(
        _norm_mod_kernel,
        grid=(B, S // ts),
        in_specs=[
            pl.BlockSpec((1, ts, D), lambda b, i: (b, i, 0)),
            pl.BlockSpec((1, D), lambda b, i: (0, 0)),
            pl.BlockSpec((1, 1, D), lambda b, i: (b, 0, 0)),
            pl.BlockSpec((1, 1, D), lambda b, i: (b, 0, 0)),
        ],
        out_specs=pl.BlockSpec((1, ts, D), lambda b, i: (b, i, 0)),
        out_shape=jax.ShapeDtypeStruct((B, S, D), BF16),
        compiler_params=_cparams(("parallel", "parallel")),
        name="norm_mod",
    )(x, g.reshape(1, D), scale, shift)


def _post_router_kernel(n_experts, x_ref, y_ref, gpost_ref, gate_ref, g_ref, sc_ref, sh_ref,
                        wr_hi_ref, wr_lo_ref, br_ref, x1_ref, h_ref, e_ref, p_ref):
    x1 = x_ref[0] + gate_ref[0] * _rms(y_ref[0].astype(F32), gpost_ref[...])
    x1_ref[0] = x1
    h = _norm_mod(x1, g_ref[...], sc_ref[0], sh_ref[0])
    h_ref[0] = _pack_bf16_pairs(h)
    h_hi = h.astype(BF16)
    h_lo = (h - h_hi.astype(F32)).astype(BF16)
    logits = (jnp.dot(h_hi, wr_hi_ref[...], preferred_element_type=F32)
              + jnp.dot(h_hi, wr_lo_ref[...], preferred_element_type=F32)
              + jnp.dot(h_lo, wr_hi_ref[...], preferred_element_type=F32)) + br_ref[...]
    lane = lax.broadcasted_iota(jnp.int32, logits.shape, 1)
    vals = jnp.where(lane < n_experts, logits, -jnp.inf)
    tops, idxs = [], []
    for _ in range(TOP_K):
        m = jnp.max(vals, axis=-1, keepdims=True)
        idx = jnp.min(jnp.where(vals == m, lane, LANES), axis=-1, keepdims=True)
        tops.append(m)
        idxs.append(idx)
        vals = jnp.where(lane == idx, -jnp.inf, vals)
    exps = [jnp.exp(t - tops[0]) for t in tops]
    denom = exps[0]
    for e in exps[1:]:
        denom = denom + e
    e_out = jnp.zeros(logits.shape, jnp.int32)
    p_out = jnp.zeros(logits.shape, F32)
    for k in range(TOP_K):
        e_out = jnp.where(lane == k, idxs[k], e_out)
        p_out = jnp.where(lane == k, exps[k] / denom, p_out)
    e_ref[...] = e_out
    p_ref[...] = p_out


def _post_router_call(x, y, g_post, gate, g_pre, scale, shift, w_router, b_router):
    B, S, D = x.shape
    E = w_router.shape[1]
    ts = _tile(S, 256)
    nb = S // ts
    wr = jnp.pad(w_router, ((0, 0), (0, LANES - E)))
    wr_hi = wr.astype(BF16)
    wr_lo = (wr - wr_hi.astype(F32)).astype(BF16)
    br = jnp.pad(b_router, (0, LANES - E)).reshape(1, LANES)
    blk = pl.BlockSpec((1, ts, D), lambda b, i: (b, i, 0))
    vec = pl.BlockSpec((1, D), lambda b, i: (0, 0))
    per_batch = pl.BlockSpec((1, 1, D), lambda b, i: (b, 0, 0))
    return pl.pallas_call(
        functools.partial(_post_router_kernel, E),
        grid=(B, nb),
        in_specs=[
            blk, blk, vec, per_batch, vec, per_batch, per_batch,
            pl.BlockSpec((D, LANES), lambda b, i: (0, 0)),
            pl.BlockSpec((D, LANES), lambda b, i: (0, 0)),
            pl.BlockSpec((1, LANES), lambda b, i: (0, 0)),
        ],
        out_specs=[
            blk,
            pl.BlockSpec((1, ts, D // 2), lambda b, i: (b, i, 0)),
            pl.BlockSpec((ts, LANES), lambda b, i: (b * nb + i, 0)),
            pl.BlockSpec((ts, LANES), lambda b, i: (b * nb + i, 0)),
        ],
        out_shape=[
            jax.ShapeDtypeStruct((B, S, D), F32),
            jax.ShapeDtypeStruct((B, S, D // 2), jnp.uint32),
            jax.ShapeDtypeStruct((B * S, LANES), jnp.int32),
            jax.ShapeDtypeStruct((B * S, LANES), F32),
        ],
        compiler_params=_cparams(("parallel", "parallel")),
        name="post_router",
    )(x, y.reshape(B, S, D), g_post.reshape(1, D), gate, g_pre.reshape(1, D), scale, shift,
      wr_hi, wr_lo, br)


def _mm_kernel(a_ref, w_ref, o_ref):
    o_ref[...] = jnp.dot(a_ref[...], w_ref[0], preferred_element_type=F32).astype(o_ref.dtype)


def _mm(a, w, layer, out_dtype, tm_pref=1024, tn_pref=512, name="mm"):
    M, K = a.shape
    N = w.shape[2]
    tm = _tile(M, tm_pref)
    tn = _tile(N, tn_pref)
    return pl.pallas_call(
        _mm_kernel,
        grid=(M // tm, N // tn),
        in_specs=[
            pl.BlockSpec((tm, K), lambda i, j: (i, 0)),
            pl.BlockSpec((1, K, tn), lambda i, j: (layer, 0, j)),
        ],
        out_specs=pl.BlockSpec((tm, tn), lambda i, j: (i, j)),
        out_shape=jax.ShapeDtypeStruct((M, N), out_dtype),
        compiler_params=_cparams(("parallel", "parallel"), 56),
        name=name,
    )(a, w)


def _conv_kernel(b_ref, c_ref, h_ref, w_ref, o_ref):
    g = c_ref[...].astype(F32) * h_ref[...].astype(F32)
    row = lax.broadcasted_iota(jnp.int32, g.shape, 0)
    g1 = jnp.where(row >= 1, pltpu.roll(g, 1, 0), 0.0)
    g2 = jnp.where(row >= 2, pltpu.roll(g, 2, 0), 0.0)
    w = w_ref[...]
    conv = w[0:1] * g2 + w[1:2] * g1 + w[2:3] * g
    o_ref[...] = (b_ref[...].astype(F32) * conv).astype(o_ref.dtype)


def _conv_call(u, conv_w, B, S, C):
    tc = _tile(C, 256)
    nc = C // tc
    w8 = jnp.pad(conv_w, ((0, 8 - CONV_K), (0, 0)))
    return pl.pallas_call(
        _conv_kernel,
        grid=(B, nc),
        in_specs=[
            pl.BlockSpec((S, tc), lambda b, j: (b, j)),
            pl.BlockSpec((S, tc), lambda b, j: (b, nc + j)),
            pl.BlockSpec((S, tc), lambda b, j: (b, 2 * nc + j)),
            pl.BlockSpec((8, tc), lambda b, j: (0, j)),
        ],
        out_specs=pl.BlockSpec((S, tc), lambda b, j: (b, j)),
        out_shape=jax.ShapeDtypeStruct((B * S, C), BF16),
        compiler_params=_cparams(("parallel", "parallel")),
        name="gated_conv",
    )(u, u, u, w8)


def _rms(x, g):
    return x * lax.rsqrt(jnp.mean(x * x, axis=-1, keepdims=True) + NORM_EPS) * g


def _mla_q_kernel(n_heads, h_ref, wc_ref, g_ref, wa_ref, wb_ref, cos_ref, sin_ref, q_ref):
    c_q = jnp.dot(h_ref[...], wc_ref[...], preferred_element_type=F32)
    n = _rms(c_q, g_ref[...]).astype(BF16)
    a = jnp.dot(n, wa_ref[...], preferred_element_type=F32)
    bm = jnp.dot(n, wb_ref[...], preferred_element_type=F32)
    cos = cos_ref[...]
    sin = sin_ref[...]
    for hd in range(n_heads):
        o = hd * QK_PAD
        q_ref[:, o:o + LANES] = a[:, o:o + LANES].astype(q_ref.dtype)
        rot = a[:, o + LANES:o + QK_PAD] * cos + bm[:, hd * LANES:(hd + 1) * LANES] * sin
        q_ref[:, o + LANES:o + QK_PAD] = rot.astype(q_ref.dtype)


def _mla_q_call(h, w_cq, q_norm_g, w_a, w_b, cos_t, sin_t, n_heads):
    M, D = h.shape
    R = w_cq.shape[1]
    tm = _tile(M, 512)
    const = lambda i: (0, 0)
    return pl.pallas_call(
        functools.partial(_mla_q_kernel, n_heads),
        grid=(M // tm,),
        in_specs=[
            pl.BlockSpec((tm, D), lambda i: (i, 0)),
            pl.BlockSpec((D, R), const),
            pl.BlockSpec((1, R), const),
            pl.BlockSpec((R, n_heads * QK_PAD), const),
            pl.BlockSpec((R, n_heads * LANES), const),
            pl.BlockSpec((tm, LANES), lambda i: (i, 0)),
            pl.BlockSpec((tm, LANES), lambda i: (i, 0)),
        ],
        out_specs=pl.BlockSpec((tm, n_heads * QK_PAD), lambda i: (i, 0)),
        out_shape=jax.ShapeDtypeStruct((M, n_heads * QK_PAD), BF16),
        compiler_params=_cparams(("parallel",), 56),
        name="mla_q",
    )(h, w_cq, q_norm_g.reshape(1, R), w_a, w_b, cos_t, sin_t)


def _mla_kv_kernel(n_heads, h_ref, wc_ref, g_ref, wk_ref, wv_ref, wra_ref, wrb_ref,
                   cos_ref, sin_ref, k_ref, v_ref):
    hb = h_ref[...]
    c_kv = jnp.dot(hb, wc_ref[...], preferred_element_type=F32)
    n = _rms(c_kv, g_ref[...]).astype(BF16)
    k_nope = jnp.dot(n, wk_ref[...], preferred_element_type=F32)
    v_ref[...] = jnp.dot(n, wv_ref[...], preferred_element_type=F32).astype(v_ref.dtype)
    ra = jnp.dot(hb, wra_ref[...], preferred_element_type=F32)
    rb = jnp.dot(hb, wrb_ref[...], preferred_element_type=F32)
    k_rope = (ra * cos_ref[...] + rb * sin_ref[...]).astype(k_ref.dtype)
    for hd in range(n_heads):
        o = hd * QK_PAD
        k_ref[:, o:o + LANES] = k_nope[:, hd * LANES:(hd + 1) * LANES].astype(k_ref.dtype)
        k_ref[:, o + LANES:o + QK_PAD] = k_rope


def _mla_kv_call(h, w_ckv, kv_norm_g, w_k, w_v, w_ra, w_rb, cos_t, sin_t, n_heads):
    M, D = h.shape
    R = w_ckv.shape[1]
    tm = _tile(M, 512)
    const = lambda i: (0, 0)
    return pl.pallas_call(
        functools.partial(_mla_kv_kernel, n_heads),
        grid=(M // tm,),
        in_specs=[
            pl.BlockSpec((tm, D), lambda i: (i, 0)),
            pl.BlockSpec((D, R), const),
            pl.BlockSpec((1, R), const),
            pl.BlockSpec((R, n_heads * LANES), const),
            pl.BlockSpec((R, n_heads * LANES), const),
            pl.BlockSpec((D, LANES), const),
            pl.BlockSpec((D, LANES), const),
            pl.BlockSpec((tm, LANES), lambda i: (i, 0)),
            pl.BlockSpec((tm, LANES), lambda i: (i, 0)),
        ],
        out_specs=[
            pl.BlockSpec((tm, n_heads * QK_PAD), lambda i: (i, 0)),
            pl.BlockSpec((tm, n_heads * LANES), lambda i: (i, 0)),
        ],
        out_shape=[
            jax.ShapeDtypeStruct((M, n_heads * QK_PAD), BF16),
            jax.ShapeDtypeStruct((M, n_heads * LANES), BF16),
        ],
        compiler_params=_cparams(("parallel",), 56),
        name="mla_kv",
    )(h, w_ckv, kv_norm_g.reshape(1, R), w_k, w_v, w_ra, w_rb, cos_t, sin_t)


def _fox_prep_kernel(n_heads, inv_scale, h_ref, wf_ref, bf_ref, q_ref, k_ref,
                     qo_ref, ko_ref, carry_ref):
    @pl.when(pl.program_id(1) == 0)
    def _():
        carry_ref[...] = jnp.zeros_like(carry_ref)

    f = jnp.dot(h_ref[...], wf_ref[...], preferred_element_type=F32) + bf_ref[...]
    log_f = jax.nn.log_sigmoid(f)
    ts = log_f.shape[0]
    row = lax.broadcasted_iota(jnp.int32, (ts, ts), 0)
    col = lax.broadcasted_iota(jnp.int32, (ts, ts), 1)
    tri = jnp.where(col <= row, 1.0, 0.0).astype(BF16)
    cum = carry_ref[0:1, :]
    for part in _split3(log_f):
        cum = cum + jnp.dot(tri, part, preferred_element_type=F32)
    carry_ref[...] = jnp.broadcast_to(cum[ts - 1:ts, :], carry_ref.shape)

    lane = lax.broadcasted_iota(jnp.int32, (ts, LANES), 1)
    cum_s = cum * inv_scale
    for hd in range(n_heads):
        colv = jnp.broadcast_to(cum_s[:, hd:hd + 1], (ts, LANES))
        hi, mid, lo = (part.astype(F32) for part in _split3(colv))
        q_aug = jnp.where(lane == 0, hi, jnp.where(lane == 1, mid, jnp.where(
            lane == 2, lo, jnp.where(lane < 6, 1.0, 0.0))))
        k_aug = jnp.where(lane < 3, 1.0, jnp.where(lane == 3, -hi, jnp.where(
            lane == 4, -mid, jnp.where(lane == 5, -lo, 0.0))))
        o = hd * QK_PAD
        qo_ref[:, o:o + LANES] = q_ref[:, hd * LANES:(hd + 1) * LANES]
        qo_ref[:, o + LANES:o + QK_PAD] = q_aug.astype(BF16)
        ko_ref[:, o:o + LANES] = k_ref[:, hd * LANES:(hd + 1) * LANES]
        ko_ref[:, o + LANES:o + QK_PAD] = k_aug.astype(BF16)


def _fox_prep_call(h, u, w_f, b_f, B, S, n_heads, q_blk, k_blk, scale):
    M, D = h.shape
    W = n_heads * LANES
    ts = _tile(S, 512)
    nb = S // ts
    row = lambda b, i: (b * nb + i, 0)
    return pl.pallas_call(
        functools.partial(_fox_prep_kernel, n_heads, 1.0 / scale),
        grid=(B, nb),
        in_specs=[
            pl.BlockSpec((ts, D), row),
            pl.BlockSpec((D, LANES), lambda b, i: (0, 0)),
            pl.BlockSpec((1, LANES), lambda b, i: (0, 0)),
            pl.BlockSpec((ts, W), lambda b, i: (b * nb + i, q_blk)),
            pl.BlockSpec((ts, W), lambda b, i: (b * nb + i, k_blk)),
        ],
        out_specs=[
            pl.BlockSpec((ts, n_heads * QK_PAD), row),
            pl.BlockSpec((ts, n_heads * QK_PAD), row),
        ],
        out_shape=[
            jax.ShapeDtypeStruct((M, n_heads * QK_PAD), BF16),
            jax.ShapeDtypeStruct((M, n_heads * QK_PAD), BF16),
        ],
        scratch_shapes=[pltpu.VMEM((8, LANES), F32)],
        compiler_params=_cparams(("parallel", "arbitrary")),
        name="fox_prep",
    )(h, w_f, b_f, u, u)


def _flash_kernel(scale_log2, t, hp, q_ref, k_ref, v_ref, o_ref, m_ref, l_ref, acc_ref):
    i = pl.program_id(2)
    m_ref[...] = jnp.full_like(m_ref, -jnp.inf)
    l_ref[...] = jnp.zeros_like(l_ref)
    acc_ref[...] = jnp.zeros_like(acc_ref)

    def tile(j, masked):
        off = pl.multiple_of(j * t, t)
        for hh in range(hp):
            q = q_ref[:, hh * QK_PAD:(hh + 1) * QK_PAD]
            k = k_ref[pl.ds(off, t), hh * QK_PAD:(hh + 1) * QK_PAD]
            v = v_ref[pl.ds(off, t), hh * HEAD_DIM:(hh + 1) * HEAD_DIM]
            s = lax.dot_general(q, k, (((1,), (1,)), ((), ())),
                                preferred_element_type=F32) * scale_log2
            if masked:
                row = lax.broadcasted_iota(jnp.int32, s.shape, 0)
                col = lax.broadcasted_iota(jnp.int32, s.shape, 1)
                s = jnp.where(col <= row, s, -jnp.inf)
            parts = [s[:, c * LANES:(c + 1) * LANES] for c in range(t // LANES)]
            part_max = parts[0]
            for part in parts[1:]:
                part_max = jnp.maximum(part_max, part)
            m_prev = m_ref[hh]
            m_new = jnp.maximum(m_prev, jnp.max(part_max, axis=-1, keepdims=True))
            alpha = jnp.exp2(m_prev - m_new)
            probs = [jnp.exp2(part - m_new) for part in parts]
            l_part = probs[0]
            for pr in probs[1:]:
                l_part = l_part + pr
            l_ref[hh] = alpha * l_ref[hh] + l_part
            p = jnp.concatenate(probs, axis=1).astype(v.dtype)
            acc_ref[hh] = alpha * acc_ref[hh] + jnp.dot(p, v, preferred_element_type=F32)
            m_ref[hh] = m_new

    def body(j, c):
        tile(j, False)
        return c

    lax.fori_loop(0, i, body, 0)
    tile(i, True)
    for hh in range(hp):
        l_row = jnp.sum(l_ref[hh], axis=-1, keepdims=True)
        o_ref[:, hh * HEAD_DIM:(hh + 1) * HEAD_DIM] = (acc_ref[hh] / l_row).astype(o_ref.dtype)


def _flash_call(q, k, v, B, S, n_heads, scale, v_blk0=0, name="flash"):
    M = q.shape[0]
    t = _tile(S, 1024)
    nq = S // t
    hp = 2
    assert n_heads % hp == 0 and v_blk0 % hp == 0
    return pl.pallas_call(
        functools.partial(_flash_kernel, scale * 1.4426950408889634, t, hp),
        grid=(B, n_heads // hp, nq),
        in_specs=[
            pl.BlockSpec((t, hp * QK_PAD), lambda b, h, i: (b * nq + i, h)),
            pl.BlockSpec((S, hp * QK_PAD), lambda b, h, i: (b, h)),
            pl.BlockSpec((S, hp * HEAD_DIM), lambda b, h, i: (b, v_blk0 // hp + h)),
        ],
        out_specs=pl.BlockSpec((t, hp * HEAD_DIM), lambda b, h, i: (b * nq + i, h)),
        out_shape=jax.ShapeDtypeStruct((M, n_heads * HEAD_DIM), BF16),
        scratch_shapes=[
            pltpu.VMEM((hp, t, LANES), F32),
            pltpu.VMEM((hp, t, LANES), F32),
            pltpu.VMEM((hp, t, HEAD_DIM), F32),
        ],
        compiler_params=_cparams(("parallel", "parallel", "arbitrary")),
        name=name,
    )(q, k, v)


def _merge_kernel(n_br, h_ref, wg_ref, bg_ref, *rest):
    br_refs = rest[:n_br]
    wb_ref = rest[n_br]
    o_ref = rest[n_br + 1]
    hb = h_ref[...]
    merged = None
    for n in range(n_br):
        gate = jax.nn.sigmoid(jnp.dot(hb, wg_ref[0, n], preferred_element_type=F32) + bg_ref[n])
        proj = jnp.dot(br_refs[n][...], wb_ref[0, n], preferred_element_type=F32)
        merged = gate * proj if merged is None else merged + gate * proj
    o_ref[...] = merged.astype(o_ref.dtype)


def _merge_call(h, w_gate, layer, b_gate, branches, w_branch):
    M, D = h.shape
    _, n_br, W, _ = w_branch.shape
    tm = _tile(M, 1024)
    tn = _tile(D, 256)
    in_specs = [
        pl.BlockSpec((tm, D), lambda i, j: (i, 0)),
        pl.BlockSpec((1, n_br, D, tn), lambda i, j: (layer, 0, 0, j)),
        pl.BlockSpec((n_br, 1, tn), lambda i, j: (0, 0, j)),
    ]
    in_specs += [pl.BlockSpec((tm, W), lambda i, j: (i, 0)) for _ in range(n_br)]
    in_specs += [pl.BlockSpec((1, n_br, W, tn), lambda i, j: (layer, 0, 0, j))]
    return pl.pallas_call(
        functools.partial(_merge_kernel, n_br),
        grid=(M // tm, D // tn),
        in_specs=in_specs,
        out_specs=pl.BlockSpec((tm, tn), lambda i, j: (i, j)),
        out_shape=jax.ShapeDtypeStruct((M, D), BF16),
        compiler_params=_cparams(("parallel", "parallel"), 56),
        name="gated_merge",
    )(h, w_gate, b_gate.reshape(n_br, 1, D), *branches, w_branch)


def _combine_post_kernel(ts, n_steps, dest_ref, x_ref, y_hbm, p_ref, g_ref, gate_ref, o_ref,
                         ybuf, sems):
    s = pl.program_id(0)

    def issue(step, slot):
        def tok_body(r, c):
            t = step * ts + r
            for k in range(TOP_K):
                pltpu.make_async_copy(y_hbm.at[pl.ds(dest_ref[t * TOP_K + k], 1)],
                                      ybuf.at[slot, k, pl.ds(r, 1)], sems.at[slot]).start()
            return c
        lax.fori_loop(0, ts, tok_body, 0)

    @pl.when(s == 0)
    def _():
        issue(0, 0)

    @pl.when(s + 1 < n_steps)
    def _():
        issue(s + 1, (s + 1) % 2)

    slot = s % 2
    pltpu.make_async_copy(ybuf.at[slot], ybuf.at[slot], sems.at[slot]).wait()
    p = p_ref[...]
    y = _unpack_bf16_pairs(ybuf[slot, 0]) * p[:, 0:1]
    for k in range(1, TOP_K):
        y = y + _unpack_bf16_pairs(ybuf[slot, k]) * p[:, k:k + 1]
    o_ref[0] = x_ref[0] + gate_ref[0] * _rms(y, g_ref[...])


def _combine_post_call(x, y_grouped, dest_flat, top_p, g, gate):
    B, S, D = x.shape
    ts = _tile(S, 128)
    nb = S // ts
    blk = pl.BlockSpec((1, ts, D), lambda s, d: (s // nb, s % nb, 0))
    grid_spec = pltpu.PrefetchScalarGridSpec(
        num_scalar_prefetch=1,
        grid=(B * nb,),
        in_specs=[blk,
                  pl.BlockSpec(memory_space=pl.ANY),
                  pl.BlockSpec((ts, LANES), lambda s, d: (s, 0)),
                  pl.BlockSpec((1, D), lambda s, d: (0, 0)),
                  pl.BlockSpec((1, 1, D), lambda s, d: (s // nb, 0, 0))],
        out_specs=blk,
        scratch_shapes=[pltpu.VMEM((2, TOP_K, ts, D // 2), jnp.uint32),
                        pltpu.SemaphoreType.DMA((2,))],
    )
    return pl.pallas_call(
        functools.partial(_combine_post_kernel, ts, B * nb),
        grid_spec=grid_spec,
        out_shape=jax.ShapeDtypeStruct((B, S, D), F32),
        compiler_params=_cparams(("arbitrary",)),
        name="combine_post",
    )(dest_flat, x, y_grouped, top_p, g.reshape(1, D), gate)


def _rank_kernel(e_ref, rank_ref, cnt_ref, carry_ref):
    @pl.when(pl.program_id(0) == 0)
    def _():
        carry_ref[...] = jnp.zeros_like(carry_ref)

    e = e_ref[...]
    tb = e.shape[0]
    lane = lax.broadcasted_iota(jnp.int32, e.shape, 1)
    sel = [jnp.broadcast_to(e[:, k:k + 1], e.shape) == lane for k in range(TOP_K)]
    onehot = jnp.zeros(e.shape, F32)
    for s in sel:
        onehot = onehot + jnp.where(s, 1.0, 0.0)
    row = lax.broadcasted_iota(jnp.int32, (tb, tb), 0)
    col = lax.broadcasted_iota(jnp.int32, (tb, tb), 1)
    strict = jnp.where(col < row, 1.0, 0.0).astype(BF16)
    before = carry_ref[0:1, :] + jnp.dot(strict, onehot.astype(BF16), preferred_element_type=F32)
    rank = jnp.zeros(e.shape, F32)
    for k in range(TOP_K):
        r_k = jnp.sum(jnp.where(sel[k], before, 0.0), axis=-1, keepdims=True)
        rank = jnp.where(lane == k, r_k, rank)
    rank_ref[...] = rank.astype(jnp.int32)
    total = carry_ref[0:1, :] + jnp.sum(onehot, axis=0, keepdims=True)
    carry_ref[...] = jnp.broadcast_to(total, carry_ref.shape)
    cnt_ref[...] = jnp.broadcast_to(total, cnt_ref.shape).astype(jnp.int32)


def _rank_call(top_e):
    N = top_e.shape[0]
    tb = _tile(N, 512)
    return pl.pallas_call(
        _rank_kernel,
        grid=(N // tb,),
        in_specs=[pl.BlockSpec((tb, LANES), lambda i: (i, 0))],
        out_specs=[pl.BlockSpec((tb, LANES), lambda i: (i, 0)),
                   pl.BlockSpec((8, LANES), lambda i: (0, 0))],
        out_shape=[jax.ShapeDtypeStruct((N, LANES), jnp.int32),
                   jax.ShapeDtypeStruct((8, LANES), jnp.int32)],
        scratch_shapes=[pltpu.VMEM((8, LANES), F32)],
        compiler_params=_cparams(("arbitrary",)),
        name="route_rank",
    )(top_e)


def _row_copy(src, dst, s_row, d_row, sem):
    return pltpu.make_async_copy(src.at[pl.ds(s_row, 1)], dst.at[pl.ds(d_row, 1)], sem)


def _rows_wait(ref, n, sem):
    pltpu.make_async_copy(ref.at[pl.ds(0, n)], ref.at[pl.ds(0, n)], sem).wait()


def _dispatch_kernel(tb, n_experts, n_blocks, dest_ref, zlo_ref, zhi_ref, nblk_ref,
                     h_ref, xg_ref, zero_ref, sem):
    i = pl.program_id(0)

    @pl.when(i == 0)
    def _():
        zero_ref[...] = jnp.zeros_like(zero_ref)

        def exp_body(e, carry):
            def issue(r, c2):
                _row_copy(zero_ref, xg_ref, 0, r, sem).start()
                return c2

            def wait(r, c2):
                _row_copy(zero_ref, xg_ref, 0, r, sem).wait()
                return c2

            lax.fori_loop(zlo_ref[e], zhi_ref[e], issue, 0)
            lax.fori_loop(zlo_ref[e], zhi_ref[e], wait, 0)
            return carry

        lax.fori_loop(0, n_experts, exp_body, 0)

        def tail_body(b, carry):
            cp = pltpu.make_async_copy(
                zero_ref,
                xg_ref.at[pl.ds(pl.multiple_of(b * EXPERT_ROWS, EXPERT_ROWS), EXPERT_ROWS)], sem)
            cp.start()
            cp.wait()
            return carry

        lax.fori_loop(nblk_ref[0], n_blocks, tail_body, 0)

    def tok_body(r, carry):
        t = i * tb + r
        for k in range(TOP_K):
            _row_copy(h_ref, xg_ref, r, dest_ref[t * TOP_K + k], sem).start()
        return carry

    lax.fori_loop(0, tb, tok_body, 0)
    _rows_wait(xg_ref, tb * TOP_K, sem)


def _dispatch_call(h, dest_flat, zlo, zhi, n_used, n_rows):
    N, W = h.shape
    E = zlo.shape[0]
    tb = _tile(N, 256)
    grid_spec = pltpu.PrefetchScalarGridSpec(
        num_scalar_prefetch=4,
        grid=(N // tb,),
        in_specs=[pl.BlockSpec((tb, W), lambda i, *_: (i, 0))],
        out_specs=pl.BlockSpec(memory_space=pl.ANY),
        scratch_shapes=[pltpu.VMEM((EXPERT_ROWS, W), h.dtype), pltpu.SemaphoreType.DMA(())],
    )
    return pl.pallas_call(
        functools.partial(_dispatch_kernel, tb, E, n_rows // EXPERT_ROWS),
        grid_spec=grid_spec,
        out_shape=jax.ShapeDtypeStruct((n_rows, W), h.dtype),
        compiler_params=pltpu.CompilerParams(dimension_semantics=("arbitrary",),
                                             has_side_effects=True),
        name="moe_dispatch",
    )(dest_flat, zlo, zhi, n_used, h)


def _expert_kernel(f_dim, be_ref, nu_ref, x_ref, wgu_ref, bgu_ref, wd_ref, bd_ref, o_ref):
    i = pl.program_id(0)

    @pl.when(i < nu_ref[0])
    def _():
        xb = _unpack_bf16_pairs(x_ref[...]).astype(BF16)
        gu = jnp.dot(xb, wgu_ref[0, 0], preferred_element_type=F32) + bgu_ref[0]
        g = jnp.minimum(gu[:, :f_dim], SWIGLU_LIMIT)
        up = jnp.clip(gu[:, f_dim:], -SWIGLU_LIMIT, SWIGLU_LIMIT)
        act = g * jax.nn.sigmoid(SWIGLU_ALPHA * g) * (up + 1.0)
        y = jnp.dot(act.astype(BF16), wd_ref[0, 0], preferred_element_type=F32) + bd_ref[0]
        o_ref[...] = _pack_bf16_pairs(y)

    @pl.when(i >= nu_ref[0])
    def _():
        o_ref[...] = jnp.zeros_like(o_ref)


def _expert_call(x_grouped, block_e, n_used, w_gu, layer, b_gu, w_d, b_d):
    _, E, D, F2 = w_gu.shape
    n_rows = x_grouped.shape[0]
    F = F2 // 2
    n_blocks = n_rows // EXPERT_ROWS
    grid_spec = pltpu.PrefetchScalarGridSpec(
        num_scalar_prefetch=2,
        grid=(n_blocks,),
        in_specs=[
            pl.BlockSpec((EXPERT_ROWS, D // 2), lambda i, be, nu: (jnp.minimum(i, nu[0] - 1), 0)),
            pl.BlockSpec((1, 1, D, F2), lambda i, be, nu: (layer, be[i], 0, 0)),
            pl.BlockSpec((1, 1, F2), lambda i, be, nu: (be[i], 0, 0)),
            pl.BlockSpec((1, 1, F, D), lambda i, be, nu: (layer, be[i], 0, 0)),
            pl.BlockSpec((1, 1, D), lambda i, be, nu: (be[i], 0, 0)),
        ],
        out_specs=pl.BlockSpec((EXPERT_ROWS, D // 2), lambda i, be, nu: (i, 0)),
    )
    return pl.pallas_call(
        functools.partial(_expert_kernel, F),
        grid_spec=grid_spec,
        out_shape=jax.ShapeDtypeStruct((n_rows, D // 2), jnp.uint32),
        compiler_params=_cparams(("arbitrary",), 56),
        name="experts",
    )(block_e, n_used, x_grouped, w_gu, b_gu.reshape(E, 1, F2), w_d, b_d.reshape(E, 1, D))


def _mixer_weights(D, w_mix_in, w_uq, w_ukv, fox_f_bias):
    Wb = D // 4
    H = Wb // HEAD_DIM
    q_lora = 3 * D // 16
    kv_lora = D // 8
    half = ROPE_DIM // 2
    o_cq = 3 * Wb
    o_ckv = o_cq + q_lora
    o_kr = o_ckv + kv_lora
    o_fox = o_kr + ROPE_DIM
    o_f = o_fox + 3 * Wb
    w_convfox = jnp.concatenate([w_mix_in[:, :o_cq], w_mix_in[:, o_fox:o_f]], axis=1)
    w_cq = w_mix_in[:, o_cq:o_ckv]
    w_ckv = w_mix_in[:, o_ckv:o_kr]
    w_kr = w_mix_in[:, o_kr:o_fox]
    zpad = jnp.zeros((D, LANES - ROPE_DIM), BF16)
    w_kra = jnp.concatenate([w_kr, zpad], axis=1)
    w_krb = jnp.concatenate([-w_kr[:, half:], w_kr[:, :half], zpad], axis=1)
    w_f = jnp.pad(w_mix_in[:, o_f:o_f + H], ((0, 0), (0, LANES - H)))
    b_f = jnp.pad(fox_f_bias, (0, LANES - H)).reshape(1, LANES)

    uq = w_uq.reshape(q_lora, H, HEAD_DIM + ROPE_DIM)
    zq = jnp.zeros((q_lora, H, QK_PAD - HEAD_DIM - ROPE_DIM), BF16)
    w_qa = jnp.concatenate([uq, zq], axis=2).reshape(q_lora, H * QK_PAD)
    r1 = uq[:, :, HEAD_DIM:HEAD_DIM + half]
    r2 = uq[:, :, HEAD_DIM + half:]
    w_qb = jnp.concatenate([-r2, r1, zq], axis=2).reshape(q_lora, H * LANES)

    ukv = w_ukv.reshape(kv_lora, H, 2 * HEAD_DIM)
    w_k = ukv[:, :, :HEAD_DIM].reshape(kv_lora, H * HEAD_DIM)
    w_v = ukv[:, :, HEAD_DIM:].reshape(kv_lora, H * HEAD_DIM)
    return dict(w_convfox=w_convfox, w_cq=w_cq, w_ckv=w_ckv, w_kra=w_kra, w_krb=w_krb,
                w_f=w_f, b_f=b_f, w_qa=w_qa, w_qb=w_qb, w_k=w_k, w_v=w_v)


def _mixer(h, B, S, D, rope_cos, rope_sin, mw, conv_w, q_norm_g, kv_norm_g,
           w_branch, w_merge_gate, layer, b_merge_gate, w_o):
    Wb = D // 4
    H = Wb // HEAD_DIM
    u = _mm(h, mw["w_convfox"][None], 0, BF16, name="mix_in")
    o_a = _conv_call(u, conv_w, B, S, Wb)

    q_mla = _mla_q_call(h, mw["w_cq"], q_norm_g, mw["w_qa"], mw["w_qb"], rope_cos, rope_sin, H)
    k_mla, v_mla = _mla_kv_call(h, mw["w_ckv"], kv_norm_g, mw["w_k"], mw["w_v"],
                                mw["w_kra"], mw["w_krb"], rope_cos, rope_sin, H)
    o_b = _flash_call(q_mla, k_mla, v_mla, B, S, H, (HEAD_DIM + ROPE_DIM) ** -0.5, name="flash_mla")

    fox_scale = HEAD_DIM ** -0.5
    q_fox, k_fox = _fox_prep_call(h, u, mw["w_f"], mw["b_f"], B, S, H, 3, 4, fox_scale)
    o_c = _flash_call(q_fox, k_fox, u, B, S, H, fox_scale, v_blk0=5 * H, name="flash_fox")

    merged = _merge_call(h, w_merge_gate, layer, b_merge_gate, (o_a, o_b, o_c), w_branch)
    return _mm(merged, w_o, layer, BF16, name="mix_out")


def _moe(x, h, top_e, top_p, n_experts, g_post, gate, w_gate_up, layer, b_gate_up, w_down, b_down):
    B, S, D = x.shape
    N = B * S
    E = n_experts
    rank, counts8 = _rank_call(top_e)

    counts = counts8[0, :E]
    padded = (counts + EXPERT_ROWS - 1) // EXPERT_ROWS * EXPERT_ROWS
    pad_end = jnp.cumsum(padded)
    pad_start = pad_end - padded
    e_flat = top_e[:, :TOP_K].reshape(-1)
    dest = pad_start[e_flat] + rank[:, :TOP_K].reshape(-1)
    n_blocks = (N * TOP_K + E * (EXPERT_ROWS - 1) + EXPERT_ROWS - 1) // EXPERT_ROWS
    n_rows = n_blocks * EXPERT_ROWS
    block_start = jnp.arange(n_blocks, dtype=jnp.int32) * EXPERT_ROWS
    block_e = jnp.minimum(
        jnp.sum((pad_end[None, :] <= block_start[:, None]).astype(jnp.int32), axis=1), E - 1)
    n_used = (pad_end[-1:] // EXPERT_ROWS).astype(jnp.int32)

    x_grouped = _dispatch_call(h.reshape(N, D // 2), dest.astype(jnp.int32),
                               (pad_start + counts).astype(jnp.int32),
                               pad_end.astype(jnp.int32), n_used, n_rows)
    y_grouped = _expert_call(x_grouped, block_e, n_used, w_gate_up, layer, b_gate_up,
                             w_down, b_down)
    return _combine_post_call(x, y_grouped, dest.astype(jnp.int32), top_p, g_post, gate)


def kernel(x, c, positions, w_ada, b_ada, g_mix_pre, g_mix_post, g_ffn_pre, g_ffn_post, w_mix_in, conv_w, q_norm_g, kv_norm_g, w_uq, w_ukv, fox_f_bias, w_branch, w_merge_gate, b_merge_gate, w_o, w_router, b_router, w_gate_up, b_gate_up, w_down, b_down):
    B, S, D = x.shape
    L = w_ada.shape[0]
    N = B * S

    c8 = jnp.pad(c, ((0, 8 - B), (0, 0)))
    ada = _ada(c8, w_ada, b_ada)[:, :B].reshape(L, B, N_ADA, 1, D)

    half = ROPE_DIM // 2
    inv_freq = ROPE_THETA ** (-jnp.arange(half, dtype=F32) / half)
    ang = positions.astype(F32).reshape(N, 1) * inv_freq
    zer = jnp.zeros((N, LANES - ROPE_DIM), F32)
    rope_cos = jnp.concatenate([jnp.cos(ang), jnp.cos(ang), zer], axis=1)
    rope_sin = jnp.concatenate([jnp.sin(ang), jnp.sin(ang), zer], axis=1)

    w_mix_bf, w_uq_bf, w_ukv_bf = (w.astype(BF16) for w in (w_mix_in, w_uq, w_ukv))
    w_branch_bf, w_gate_bf, w_o_bf = (w.astype(BF16) for w in (w_branch, w_merge_gate, w_o))
    w_gu_bf, w_down_bf = w_gate_up.astype(BF16), w_down.astype(BF16)

    for l in range(L):
        shift_m, scale_m, gate_m = ada[l, :, 0], ada[l, :, 1], ada[l, :, 2]
        shift_f, scale_f, gate_f = ada[l, :, 3], ada[l, :, 4], ada[l, :, 5]

        h = _norm_mod_call(x, g_mix_pre[l], scale_m, shift_m).reshape(N, D)
        mw = _mixer_weights(D, w_mix_bf[l], w_uq_bf[l], w_ukv_bf[l], fox_f_bias[l])
        y = _mixer(h, B, S, D, rope_cos, rope_sin, mw, conv_w[l], q_norm_g[l], kv_norm_g[l],
                   w_branch_bf, w_gate_bf, l, b_merge_gate[l], w_o_bf)
        x, h_ffn, top_e, top_p = _post_router_call(x, y, g_mix_post[l], gate_m, g_ffn_pre[l],
                                                   scale_f, shift_f, w_router[l], b_router[l])
        x = _moe(x, h_ffn, top_e, top_p, w_router.shape[2], g_ffn_post[l], gate_f,
                 w_gu_bf, l, b_gate_up[l], w_down_bf, b_down[l])
    return x
```
